```python
import math
import jax, jax.numpy as jnp
from jax import lax
import numpy as np

D_MODEL = 1024
BATCH = 8
SEQ = 2048
DEPTH = 4

N_MIXERS = 3
ATTN_HEADS = 16
ATTN_KV_HEADS = 2
ATTN_GROUP = ATTN_HEADS // ATTN_KV_HEADS
HEAD_DIM = 64
WINDOW = 128
QKV_DIM = (ATTN_HEADS + 2 * ATTN_KV_HEADS) * HEAD_DIM
CONV_WIDTH = 31
HGRN_EXPAND = 128
HGRN_HEADS = D_MODEL // HGRN_EXPAND
HGRN_KDIM = HGRN_EXPAND
HGRN_VDIM = D_MODEL // HGRN_HEADS
FORGET_DIM = HGRN_HEADS * HGRN_KDIM
HGRN_CHUNK = 64
D_FF = 4 * D_MODEL
NORM_EPS = 1e-6
N_ATTN_LAYERS = (DEPTH + 2) // 3
N_CONV_LAYERS = (DEPTH + 1) // 3
N_HGRN_LAYERS = DEPTH // 3

kernel_name = "hybrid_swa_conformer_hgrn2_trunk"


def rms_norm(x, g):
    x32 = x.astype(jnp.float32)
    y = x32 * lax.rsqrt(jnp.mean(x32 * x32, axis=-1, keepdims=True) + NORM_EPS)
    return y.astype(x.dtype) * g


def layer_norm(x, g, b):
    x32 = x.astype(jnp.float32)
    mu = jnp.mean(x32, axis=-1, keepdims=True)
    var = jnp.mean(jnp.square(x32 - mu), axis=-1, keepdims=True)
    y = (x32 - mu) * lax.rsqrt(var + NORM_EPS)
    return y.astype(x.dtype) * g + b


def alibi_slopes(n_heads):
    h = jnp.arange(1, n_heads + 1, dtype=jnp.float32)
    return jnp.exp2(-8.0 * h / n_heads)


def swa_sink_attention(h, w_qkv, sinks, w_o):
    B, T, _ = h.shape
    nb = T // WINDOW
    qkv = h @ w_qkv
    q, k, v = jnp.split(qkv, [ATTN_HEADS * HEAD_DIM, (ATTN_HEADS + ATTN_KV_HEADS) * HEAD_DIM], axis=-1)
    q = q.reshape(B, nb, WINDOW, ATTN_KV_HEADS, ATTN_GROUP, HEAD_DIM)

    def banded(z):
        z = z.reshape(B, T, ATTN_KV_HEADS, HEAD_DIM)
        zp = jnp.pad(z, ((0, 0), (WINDOW, 0), (0, 0), (0, 0)))
        prev = zp[:, :T].reshape(B, nb, WINDOW, ATTN_KV_HEADS, HEAD_DIM)
        cur = zp[:, WINDOW:].reshape(B, nb, WINDOW, ATTN_KV_HEADS, HEAD_DIM)
        return jnp.concatenate([prev, cur], axis=2)

    kb, vb = banded(k), banded(v)
    scale = HEAD_DIM ** -0.5
    scores = jnp.einsum('bnqkgd,bnskd->bkgnqs', q.astype(jnp.float32), kb.astype(jnp.float32)) * scale
    qi = jnp.arange(WINDOW)[:, None]
    si = jnp.arange(2 * WINDOW)[None, :]
    dist = WINDOW + qi - si
    kpos = (jnp.arange(nb)[:, None] - 1) * WINDOW + jnp.arange(2 * WINDOW)[None, :]
    valid = ((dist >= 0) & (dist < WINDOW))[None] & (kpos >= 0)[:, None, :]
    slopes = alibi_slopes(ATTN_HEADS).reshape(ATTN_KV_HEADS, ATTN_GROUP, 1, 1, 1)
    scores = scores - slopes * dist.astype(jnp.float32)
    scores = jnp.where(valid, scores, -1e30)
    sink = jnp.broadcast_to(sinks.astype(jnp.float32).reshape(1, ATTN_KV_HEADS, ATTN_GROUP, 1, 1, 1),
                            scores.shape[:-1] + (1,))
    probs = jax.nn.softmax(jnp.concatenate([scores, sink], axis=-1), axis=-1)[..., :-1]
    out = jnp.einsum('bkgnqs,bnskd->bnqkgd', probs.astype(vb.dtype), vb)
    out = out.reshape(B, T, ATTN_HEADS * HEAD_DIM)
    return out @ w_o


def conformer_conv(h, w_pw1, b_pw1, w_dw, b_dw, ln_g, ln_b, w_pw2, b_pw2):
    u = h @ w_pw1 + b_pw1
    a, gate = jnp.split(u, 2, axis=-1)
    u = a * jax.nn.sigmoid(gate)
    up = jnp.pad(u, ((0, 0), (CONV_WIDTH - 1, 0), (0, 0)))
    c = lax.conv_general_dilated(up, w_dw[:, None, :].astype(up.dtype), window_strides=(1,), padding='VALID',
                                 dimension_numbers=('NWC', 'WIO', 'NWC'), feature_group_count=D_MODEL)
    c = c + b_dw
    c = jax.nn.silu(layer_norm(c, ln_g, ln_b))
    return c @ w_pw2 + b_pw2


def hgrn2(h, w_qfig, lower_bound, norm_g, w_o):
    B, T, _ = h.shape
    nc = T // HGRN_CHUNK
    q, f, i, g = jnp.split(h @ w_qfig, [FORGET_DIM, 2 * FORGET_DIM, 2 * FORGET_DIM + D_MODEL], axis=-1)
    q = jax.nn.silu(q.astype(jnp.float32)) * (HGRN_KDIM ** -0.5)
    f = lower_bound + (1.0 - lower_bound) * jax.nn.sigmoid(f.astype(jnp.float32))
    k = 1.0 - f
    logf = jnp.log(f)

    def to_chunks(z, dim):
        z = z.astype(jnp.float32).reshape(B, nc, HGRN_CHUNK, HGRN_HEADS, dim)
        return z.transpose(1, 0, 3, 2, 4)

    qc, kc, gc = to_chunks(q, HGRN_KDIM), to_chunks(k, HGRN_KDIM), to_chunks(logf, HGRN_KDIM)
    vc = to_chunks(i, HGRN_VDIM)
    causal = jnp.tril(jnp.ones((HGRN_CHUNK, HGRN_CHUNK), dtype=bool))[:, :, None]

    def step(S, inp):
        q_, k_, v_, g_ = inp
        b = jnp.cumsum(g_, axis=2)
        diff = b[:, :, :, None, :] - b[:, :, None, :, :]
        decay = jnp.exp(jnp.where(causal, diff, -jnp.inf))
        A = jnp.einsum('bhtk,bhsk,bhtsk->bhts', q_, k_, decay)
        o = jnp.einsum('bhts,bhsv->bhtv', A, v_) + jnp.einsum('bhtk,bhkv->bhtv', q_ * jnp.exp(b), S)
        b_last = b[:, :, -1:, :]
        S_new = jnp.exp(b_last[:, :, 0, :])[..., None] * S + jnp.einsum('bhsk,bhsv->bhkv', k_ * jnp.exp(b_last - b), v_)
        return S_new, o

    S0 = jnp.zeros((B, HGRN_HEADS, HGRN_KDIM, HGRN_VDIM), jnp.float32)
    _, o = lax.scan(step, S0, (qc, kc, vc, gc))
    o = o.transpose(1, 0, 3, 2, 4).reshape(B, T, HGRN_HEADS, HGRN_VDIM)
    o = o * lax.rsqrt(jnp.mean(o * o, axis=-1, keepdims=True) + NORM_EPS)
    o = o.reshape(B, T, D_MODEL).astype(h.dtype) * norm_g * jax.nn.silu(g)
    return o @ w_o


def squared_relu_mlp(h, w1, w2):
    return jnp.square(jax.nn.relu(h @ w1)) @ w2


def setup_inputs(seed: int = 0) -> dict:
    key = jax.random.key(seed)
    ks = jax.random.split(key, 24)
    nrm = lambda k, shape, s: jax.random.normal(k, shape, jnp.float32) * s
    D = D_MODEL
    return {
        "x": nrm(ks[0], (BATCH, SEQ, D), 1.0),
        "attn_w_qkv": nrm(ks[1], (N_ATTN_LAYERS, D, QKV_DIM), D ** -0.5),
        "attn_sinks": nrm(ks[2], (N_ATTN_LAYERS, ATTN_HEADS), 0.5),
        "attn_w_o": nrm(ks[3], (N_ATTN_LAYERS, ATTN_HEADS * HEAD_DIM, D), (ATTN_HEADS * HEAD_DIM) ** -0.5),
        "conv_w_pw1": nrm(ks[4], (N_CONV_LAYERS, D, 2 * D), D ** -0.5),
        "conv_b_pw1": nrm(ks[5], (N_CONV_LAYERS, 2 * D), 0.02),
        "conv_w_dw": nrm(ks[6], (N_CONV_LAYERS, CONV_WIDTH, D), CONV_WIDTH ** -0.5),
        "conv_b_dw": nrm(ks[7], (N_CONV_LAYERS, D), 0.02),
        "conv_ln_g": 1.0 + nrm(ks[8], (N_CONV_LAYERS, D), 0.02),
        "conv_ln_b": nrm(ks[9], (N_CONV_LAYERS, D), 0.02),
        "conv_w_pw2": nrm(ks[10], (N_CONV_LAYERS, D, D), D ** -0.5),
        "conv_b_pw2": nrm(ks[11], (N_CONV_LAYERS, D), 0.02),
        "hgrn_w_qfig": nrm(ks[12], (N_HGRN_LAYERS, D, 2 * FORGET_DIM + 2 * D), D ** -0.5),
        "hgrn_lower_bounds": nrm(ks[13], (DEPTH, FORGET_DIM), 0.1),
        "hgrn_norm_g": 1.0 + nrm(ks[14], (N_HGRN_LAYERS, D), 0.02),
        "hgrn_w_o": nrm(ks[15], (N_HGRN_LAYERS, D, D), D ** -0.5),
        "norm_mixer": 1.0 + nrm(ks[16], (DEPTH, D), 0.02),
        "norm_mlp": 1.0 + nrm(ks[17], (DEPTH, D), 0.02),
        "mlp_w1": nrm(ks[18], (DEPTH, D, D_FF), D ** -0.5),
        "mlp_w2": nrm(ks[19], (DEPTH, D_FF, D), D_FF ** -0.5),
        "final_norm": 1.0 + nrm(ks[20], (D,), 0.02),
    }


def reference(x, attn_w_qkv, attn_sinks, attn_w_o, conv_w_pw1, conv_b_pw1, conv_w_dw, conv_b_dw,
              conv_ln_g, conv_ln_b, conv_w_pw2, conv_b_pw2, hgrn_w_qfig, hgrn_lower_bounds, hgrn_norm_g,
              hgrn_w_o, norm_mixer, norm_mlp, mlp_w1, mlp_w2, final_norm):
    lb = jax.nn.softmax(hgrn_lower_bounds.astype(jnp.float32), axis=0)
    lb = jnp.cumsum(lb, axis=0) - lb[0:1]
    h = x
    for i in range(DEPTH):
        kind = i % N_MIXERS
        j = i // N_MIXERS
        hn = rms_norm(h, norm_mixer[i])
        if kind == 0:
            y = swa_sink_attention(hn, attn_w_qkv[j], attn_sinks[j], attn_w_o[j])
        elif kind == 1:
            y = conformer_conv(hn, conv_w_pw1[j], conv_b_pw1[j], conv_w_dw[j], conv_b_dw[j],
                               conv_ln_g[j], conv_ln_b[j], conv_w_pw2[j], conv_b_pw2[j])
        else:
            y = hgrn2(hn, hgrn_w_qfig[j], lb[i], hgrn_norm_g[j], hgrn_w_o[j])
        h = h + y
        h = h + squared_relu_mlp(rms_norm(h, norm_mlp[i]), mlp_w1[i], mlp_w2[i])
    return rms_norm(h, final_norm)
```

```python
import functools

import jax
import jax.numpy as jnp
from jax import lax
from jax.experimental import pallas as pl
from jax.experimental.pallas import tpu as pltpu

F32 = jnp.float32
BF16 = jnp.bfloat16

D_MODEL = 1024
DEPTH = 4
N_MIXERS = 3
ATTN_HEADS = 16
ATTN_KV_HEADS = 2
HEAD_DIM = 64
WINDOW = 128
QKV_DIM = (ATTN_HEADS + 2 * ATTN_KV_HEADS) * HEAD_DIM
CONV_WIDTH = 31
HGRN_HEADS = 8
HGRN_DK = 128
D_FF = 4 * D_MODEL
NORM_EPS = 1e-6
MASK_VALUE = -1e30

V7X_VMEM_BYTES = 64 * 1024 * 1024
VMEM_LIMIT_BYTES = V7X_VMEM_BYTES - 8 * 1024 * 1024

ROW_TILE = 512
FF_CHUNK = 1024
CONV_TILE = 512
CONV_HALO = 32
CONV_ROWS = 32
HGRN_CHUNK = 128
SUBLANES = 8


def _params(*semantics):
    return pltpu.CompilerParams(dimension_semantics=semantics, vmem_limit_bytes=VMEM_LIMIT_BYTES)


def _resident(shape):
    return pl.BlockSpec(shape, lambda *_: (0,) * len(shape), pipeline_mode=pl.Buffered(1))


def _rms_norm(x, g):
    ms = jnp.mean(x * x, axis=-1, keepdims=True)
    return x * lax.rsqrt(ms + NORM_EPS) * g


def _dot(a, b):
    return jnp.dot(a, b, preferred_element_type=F32)


def _dot_nt(a, b):
    return lax.dot_general(a, b, (((1,), (1,)), ((), ())), preferred_element_type=F32)


def _norm_proj_kernel(x_ref, g_ref, w_ref, o_ref):
    hn = _rms_norm(x_ref[...], g_ref[...]).astype(BF16)
    n_out = o_ref.shape[1]
    chunk = min(n_out, FF_CHUNK)
    for c in range(0, n_out, chunk):
        width = min(chunk, n_out - c)
        o_ref[:, c:c + width] = _dot(hn, w_ref[:, c:c + width]).astype(o_ref.dtype)


def _norm_glu_kernel(x_ref, g_ref, w_ref, b_ref, o_ref):
    hn = _rms_norm(x_ref[...], g_ref[...]).astype(BF16)
    a = _dot(hn, w_ref[:, :D_MODEL]) + b_ref[:, :D_MODEL]
    gate = _dot(hn, w_ref[:, D_MODEL:]) + b_ref[:, D_MODEL:]
    o_ref[...] = a * jax.nn.sigmoid(gate)


def _norm_proj(h, g, w, out_dtype):
    n, d = h.shape
    n_out = w.shape[1]
    return pl.pallas_call(
        _norm_proj_kernel,
        grid=(n // ROW_TILE,),
        in_specs=[pl.BlockSpec((ROW_TILE, d), lambda i: (i, 0)), _resident((1, d)), _resident((d, n_out))],
        out_specs=pl.BlockSpec((ROW_TILE, n_out), lambda i: (i, 0)),
        out_shape=jax.ShapeDtypeStruct((n, n_out), out_dtype),
        compiler_params=_params("parallel"),
        name="norm_proj",
    )(h, g, w)


def _norm_glu(h, g, w, b):
    n, d = h.shape
    return pl.pallas_call(
        _norm_glu_kernel,
        grid=(n // ROW_TILE,),
        in_specs=[pl.BlockSpec((ROW_TILE, d), lambda i: (i, 0)), _resident((1, d)),
                  _resident((d, 2 * d)), _resident((1, 2 * d))],
        out_specs=pl.BlockSpec((ROW_TILE, d), lambda i: (i, 0)),
        out_shape=jax.ShapeDtypeStruct((n, d), F32),
        compiler_params=_params("parallel"),
        name="norm_glu",
    )(h, g, w, b)


def _attn_kernel(sink_ref, q_ref, kvc_ref, kvp_ref, o_ref):
    blk = pl.program_id(1)
    band = 2 * WINDOW
    pair = 2 * HEAD_DIM
    kv = jnp.concatenate([kvp_ref[...], kvc_ref[...]], axis=0).astype(F32)
    k2, v2 = kv[:, :pair], kv[:, pair:]
    k2r, v2r = pltpu.roll(k2, HEAD_DIM, 1), pltpu.roll(v2, HEAD_DIM, 1)
    low = lax.broadcasted_iota(jnp.int32, (band, pair), 1) < HEAD_DIM

    def block_diag(own, rolled, kv_head):
        first, second = (own, rolled) if kv_head == 0 else (rolled, own)
        top = jnp.where(low, first, 0.0)
        bottom = jnp.where(low, 0.0, second)
        return jnp.concatenate([top, bottom], axis=0).astype(BF16)

    k_bd = [block_diag(k2, k2r, kvh) for kvh in range(ATTN_KV_HEADS)]
    v_bd = [block_diag(v2, v2r, kvh) for kvh in range(ATTN_KV_HEADS)]

    qi = lax.broadcasted_iota(jnp.int32, (WINDOW, 2 * band), 0)
    col = lax.broadcasted_iota(jnp.int32, (WINDOW, 2 * band), 1)
    si = jnp.where(col >= band, col - band, col)
    dist = WINDOW + qi - si
    valid = (dist >= 0) & (dist < WINDOW) & ((si >= WINDOW) | (blk > 0))
    dist_f = dist.astype(F32)
    second_head = col >= band
    scale = HEAD_DIM ** -0.5
    group = ATTN_HEADS // ATTN_KV_HEADS

    for j in range(ATTN_HEADS // 2):
        heads = (2 * j, 2 * j + 1)
        kvh = heads[0] // group
        slopes = [2.0 ** (-8.0 * (hd + 1) / ATTN_HEADS) for hd in heads]
        q2 = q_ref[:, j * pair:(j + 1) * pair]
        s = _dot_nt(q2, k_bd[kvh])
        slope = jnp.where(second_head, slopes[1], slopes[0])
        s = jnp.where(valid, s * scale - slope * dist_f, MASK_VALUE)
        probs = []
        for half, hd in enumerate(heads):
            sh = s[:, half * band:(half + 1) * band]
            sink = sink_ref[hd]
            m = jnp.maximum(jnp.max(sh, axis=-1, keepdims=True), sink)
            e = jnp.exp(sh - m)
            denom = jnp.sum(e, axis=-1, keepdims=True) + jnp.exp(sink - m)
            probs.append(e * (1.0 / denom))
        p = jnp.concatenate(probs, axis=1).astype(BF16)
        o_ref[:, j * pair:(j + 1) * pair] = _dot(p, v_bd[kvh]).astype(o_ref.dtype)


def _attention_core(qkv, sinks, batch, seq):
    n = batch * seq
    nb = seq // WINDOW
    q_cols = ATTN_HEADS * HEAD_DIM
    kv_cols = 2 * ATTN_KV_HEADS * HEAD_DIM
    kv_col_block = q_cols // kv_cols
    return pl.pallas_call(
        _attn_kernel,
        grid=(batch, nb),
        in_specs=[
            pl.BlockSpec(memory_space=pltpu.SMEM),
            pl.BlockSpec((WINDOW, q_cols), lambda b, i: (b * nb + i, 0)),
            pl.BlockSpec((WINDOW, kv_cols), lambda b, i: (b * nb + i, kv_col_block)),
            pl.BlockSpec((WINDOW, kv_cols), lambda b, i: (b * nb + jnp.maximum(i - 1, 0), kv_col_block)),
        ],
        out_specs=pl.BlockSpec((WINDOW, q_cols), lambda b, i: (b * nb + i, 0)),
        out_shape=jax.ShapeDtypeStruct((n, q_cols), BF16),
        compiler_params=_params("parallel", "parallel"),
        name="attn_core",
    )(sinks, qkv, qkv, qkv)


def _conv_kernel(u_ref, w_ref, bdw_ref, lng_ref, lnb_ref, o_ref, buf_ref, wb_ref):
    t = pl.program_id(1)
    tile = u_ref.shape[0]

    @pl.when((pl.program_id(0) == 0) & (t == 0))
    def _():
        for j in range(CONV_WIDTH):
            wb_ref[j] = jnp.broadcast_to(w_ref[j:j + 1, :], (SUBLANES, D_MODEL))

    @pl.when(t == 0)
    def _():
        buf_ref[0, 0:CONV_HALO, :] = jnp.zeros((CONV_HALO, D_MODEL), F32)

    @pl.when(t > 0)
    def _():
        buf_ref[0, 0:CONV_HALO, :] = buf_ref[0, tile:tile + CONV_HALO, :]

    buf_ref[0, CONV_HALO:, :] = u_ref[...]
    shifted_rows = tile + CONV_HALO - SUBLANES
    for r in range(1, SUBLANES):
        buf_ref[r, 0:shifted_rows, :] = buf_ref[0, r:r + shifted_rows, :]

    first_tap_row = CONV_HALO - (CONV_WIDTH - 1)

    def rows(i, carry):
        base = pl.multiple_of(i * CONV_ROWS, CONV_ROWS)
        acc = jnp.broadcast_to(bdw_ref[...], (CONV_ROWS, D_MODEL))
        for j in range(CONV_WIDTH):
            a, r = divmod(first_tap_row + j, SUBLANES)
            win = buf_ref[r, pl.ds(base + a * SUBLANES, CONV_ROWS), :]
            wj = jnp.concatenate([wb_ref[j]] * (CONV_ROWS // SUBLANES), axis=0)
            acc = acc + wj * win
        mu = jnp.mean(acc, axis=-1, keepdims=True)
        cen = acc - mu
        var = jnp.mean(cen * cen, axis=-1, keepdims=True)
        y = cen * lax.rsqrt(var + NORM_EPS) * lng_ref[...] + lnb_ref[...]
        o_ref[pl.ds(base, CONV_ROWS), :] = (y * jax.nn.sigmoid(y)).astype(o_ref.dtype)
        return carry

    lax.fori_loop(0, tile // CONV_ROWS, rows, 0)


def _conv_core(u, w_dw, b_dw, ln_g, ln_b, batch, seq):
    n, d = u.shape
    nt = seq // CONV_TILE
    row = lambda b, t: (b * nt + t, 0)
    return pl.pallas_call(
        _conv_kernel,
        grid=(batch, nt),
        in_specs=[pl.BlockSpec((CONV_TILE, d), row), _resident((CONV_WIDTH, d)),
                  _resident((1, d)), _resident((1, d)), _resident((1, d))],
        out_specs=pl.BlockSpec((CONV_TILE, d), row),
        out_shape=jax.ShapeDtypeStruct((n, d), BF16),
        scratch_shapes=[pltpu.VMEM((SUBLANES, CONV_TILE + CONV_HALO, d), F32),
                        pltpu.VMEM((CONV_WIDTH, SUBLANES, d), F32)],
        compiler_params=_params("arbitrary", "arbitrary"),
        name="conv_core",
    )(u, w_dw, b_dw, ln_g, ln_b)


def _hgrn_constants():
    c = HGRN_CHUNK
    t = jnp.arange(c)[:, None]
    s = jnp.arange(c)[None, :]
    tri = (s <= t).astype(BF16)
    masks = [((t // (2 * hs)) == (s // (2 * hs))).astype(F32) for hs in (32, 16, 8)]
    masks.append(((t // SUBLANES) == (s // SUBLANES)).astype(F32))
    row = jnp.arange(SUBLANES * HGRN_DK)[:, None] // HGRN_DK
    lane = jnp.arange(c)[None, :] % SUBLANES
    fold = (row == lane).astype(BF16)
    return tri, jnp.stack(masks), fold


def _hgrn_kernel(x_ref, lbp_ref, ng_ref, tri_ref, mask_ref, fold_ref, o_ref,
                 q_s, k_s, b_s, st_ref, *, layer):
    chunk = HGRN_CHUNK

    @pl.when(pl.program_id(1) == 0)
    def _():
        st_ref[...] = jnp.zeros(st_ref.shape, F32)

    p = lbp_ref[...]
    e = jnp.exp(p - jnp.max(p, axis=0, keepdims=True))
    sm = e / jnp.sum(e, axis=0, keepdims=True)
    lb = jnp.zeros((1, D_MODEL), F32)
    for l in range(1, layer + 1):
        lb = lb + sm[l:l + 1, :]

    qr = x_ref[:, 0:D_MODEL]
    q_s[...] = qr * jax.nn.sigmoid(qr) * (HGRN_DK ** -0.5)
    f = lb + (1.0 - lb) * jax.nn.sigmoid(x_ref[:, D_MODEL:2 * D_MODEL])
    k_s[...] = 1.0 - f
    logf = jnp.log(f)
    g1 = logf.astype(BF16)
    r1 = logf - g1.astype(F32)
    g2 = r1.astype(BF16)
    g3 = (r1 - g2.astype(F32)).astype(BF16)
    tri = tri_ref[...]
    b_s[...] = _dot(tri, g1) + _dot(tri, g2) + _dot(tri, g3)

    t_in_blk = lax.broadcasted_iota(jnp.int32, (chunk // SUBLANES, SUBLANES, HGRN_DK), 1)

    for hd in range(HGRN_HEADS):
        sl = slice(hd * HGRN_DK, (hd + 1) * HGRN_DK)
        qh, kh, bh = q_s[:, sl], k_s[:, sl], b_s[:, sl]
        vh = x_ref[:, 2 * D_MODEL + hd * HGRN_DK:2 * D_MODEL + (hd + 1) * HGRN_DK]
        gr = x_ref[:, 3 * D_MODEL + hd * HGRN_DK:3 * D_MODEL + (hd + 1) * HGRN_DK]
        vb = vh.astype(BF16)
        b_last = bh[chunk - 1:chunk, :]
        state = st_ref[hd]

        o = _dot_nt((qh * jnp.exp(bh)).astype(BF16), state.astype(BF16))

        a = jnp.zeros((chunk, chunk), F32)
        for lvl, hs in enumerate((64, 32, 16, 8)):
            q_parts, k_parts = [], []
            zeros = jnp.zeros((hs, HGRN_DK), F32)
            for s0 in range(0, chunk, 2 * hs):
                pivot = bh[s0 + hs - 1:s0 + hs, :]
                lo, up = slice(s0, s0 + hs), slice(s0 + hs, s0 + 2 * hs)
                q_parts += [zeros, qh[up] * jnp.exp(bh[up] - pivot)]
                k_parts += [kh[lo] * jnp.exp(pivot - bh[lo]), zeros]
            al = _dot_nt(jnp.concatenate(q_parts, axis=0).astype(BF16),
                         jnp.concatenate(k_parts, axis=0).astype(BF16))
            a = a + (al if lvl == 0 else al * mask_ref[lvl - 1])

        b3 = bh.reshape(chunk // SUBLANES, SUBLANES, HGRN_DK)
        q3 = qh.reshape(chunk // SUBLANES, SUBLANES, HGRN_DK)
        k3 = kh.reshape(chunk // SUBLANES, SUBLANES, HGRN_DK)
        cols = []
        for s in range(SUBLANES):
            diff = jnp.where(t_in_blk >= s, b3 - b3[:, s:s + 1, :], MASK_VALUE)
            prod = q3 * jnp.exp(diff) * k3[:, s:s + 1, :]
            cols.append(prod.reshape(chunk, HGRN_DK).astype(BF16))
        a = a + _dot(jnp.concatenate(cols, axis=1), fold_ref[...]) * mask_ref[3]

        o = o + _dot(a.astype(BF16), vb)

        k_dec = (kh * jnp.exp(b_last - bh)).astype(BF16)
        st_ref[hd] = state * jnp.exp(b_last) + _dot(vh.T.astype(BF16), k_dec)

        on = o * lax.rsqrt(jnp.mean(o * o, axis=-1, keepdims=True) + NORM_EPS)
        o_ref[:, sl] = (on * ng_ref[:, sl] * (gr * jax.nn.sigmoid(gr))).astype(o_ref.dtype)


def _hgrn_core(qfig, lower_bounds, norm_g, layer, batch, seq):
    n = qfig.shape[0]
    nc = seq // HGRN_CHUNK
    tri, masks, fold = _hgrn_constants()
    return pl.pallas_call(
        functools.partial(_hgrn_kernel, layer=layer),
        grid=(batch, nc),
        in_specs=[pl.BlockSpec((HGRN_CHUNK, 4 * D_MODEL), lambda b, c: (b * nc + c, 0)),
                  _resident(lower_bounds.shape), _resident((1, D_MODEL)),
                  _resident(tri.shape), _resident(masks.shape), _resident(fold.shape)],
        out_specs=pl.BlockSpec((HGRN_CHUNK, D_MODEL), lambda b, c: (b * nc + c, 0)),
        out_shape=jax.ShapeDtypeStruct((n, D_MODEL), BF16),
        scratch_shapes=[pltpu.VMEM((HGRN_CHUNK, D_MODEL), F32)] * 3
                       + [pltpu.VMEM((HGRN_HEADS, HGRN_DK, HGRN_DK), F32)],
        compiler_params=_params("arbitrary", "arbitrary"),
        name="hgrn_core",
    )(qfig, lower_bounds, norm_g, tri, masks, fold)


def _resid_mlp_kernel(h_ref, y_ref, wo_ref, bo_ref, g_ref, w1_ref, w2_ref, gf_ref, o_ref, *,
                      has_bias, final_norm):
    h2 = h_ref[...] + _dot(y_ref[...], wo_ref[...])
    if has_bias:
        h2 = h2 + bo_ref[...]
    hn = _rms_norm(h2, g_ref[...]).astype(BF16)
    acc = h2
    for c in range(0, D_FF, FF_CHUNK):
        up = jnp.maximum(_dot(hn, w1_ref[:, c:c + FF_CHUNK]), 0.0)
        acc = acc + _dot((up * up).astype(BF16), w2_ref[c:c + FF_CHUNK, :])
    if final_norm:
        acc = _rms_norm(acc, gf_ref[...])
    o_ref[...] = acc


def _resid_mlp(h, y, wo, bo, g, w1, w2, gf, has_bias, final_norm):
    n, d = h.shape
    row = lambda i: (i, 0)
    return pl.pallas_call(
        functools.partial(_resid_mlp_kernel, has_bias=has_bias, final_norm=final_norm),
        grid=(n // ROW_TILE,),
        in_specs=[pl.BlockSpec((ROW_TILE, d), row), pl.BlockSpec((ROW_TILE, d), row),
                  _resident((d, d)), _resident((1, d)), _resident((1, d)),
                  _resident((d, D_FF)), _resident((D_FF, d)), _resident((1, d))],
        out_specs=pl.BlockSpec((ROW_TILE, d), row),
        out_shape=jax.ShapeDtypeStruct((n, d), F32),
        compiler_params=_params("parallel"),
        name="resid_mlp",
    )(h, y, wo, bo, g, w1, w2, gf)


def kernel(x, attn_w_qkv, attn_sinks, attn_w_o, conv_w_pw1, conv_b_pw1, conv_w_dw, conv_b_dw, conv_ln_g,
           conv_ln_b, conv_w_pw2, conv_b_pw2, hgrn_w_qfig, hgrn_lower_bounds, hgrn_norm_g, hgrn_w_o,
           norm_mixer, norm_mlp, mlp_w1, mlp_w2, final_norm):
    batch, seq, d = x.shape
    h = x.reshape(batch * seq, d)
    row = lambda v: v.reshape(1, -1)
    zero_bias = jnp.zeros((1, d), F32)
    for i in range(DEPTH):
        kind, j = i % N_MIXERS, i // N_MIXERS
        g_mix = row(norm_mixer[i])
        bias, has_bias = zero_bias, False
        if kind == 0:
            qkv = _norm_proj(h, g_mix, attn_w_qkv[j].astype(BF16), BF16)
            y = _attention_core(qkv, attn_sinks[j], batch, seq)
            wo = attn_w_o[j]
        elif kind == 1:
            u = _norm_glu(h, g_mix, conv_w_pw1[j].astype(BF16), row(conv_b_pw1[j]))
            y = _conv_core(u, conv_w_dw[j], row(conv_b_dw[j]), row(conv_ln_g[j]), row(conv_ln_b[j]), batch, seq)
            wo, bias, has_bias = conv_w_pw2[j], row(conv_b_pw2[j]), True
        else:
            qfig = _norm_proj(h, g_mix, hgrn_w_qfig[j].astype(BF16), F32)
            y = _hgrn_core(qfig, hgrn_lower_bounds, row(hgrn_norm_g[j]), i, batch, seq)
            wo = hgrn_w_o[j]
        h = _resid_mlp(h, y, wo.astype(BF16), bias, row(norm_mlp[i]), mlp_w1[i].astype(BF16),
                       mlp_w2[i].astype(BF16), row(final_norm), has_bias, i == DEPTH - 1)
    return h.reshape(batch, seq, d)
```

```python
import functools
import math

import jax
import jax.numpy as jnp
from jax import lax
from jax.experimental import pallas as pl
from jax.experimental.pallas import tpu as pltpu

F32 = jnp.float32
BF16 = jnp.bfloat16

D_MODEL = 1024
DEPTH = 4
N_MIXERS = 3
ATTN_HEADS = 16
ATTN_KV_HEADS = 2
HEAD_DIM = 64
WINDOW = 128
QKV_DIM = (ATTN_HEADS + 2 * ATTN_KV_HEADS) * HEAD_DIM
CONV_WIDTH = 31
HGRN_HEADS = 8
HGRN_DK = 128
D_FF = 4 * D_MODEL
NORM_EPS = 1e-6
MASK_VALUE = -1e30
LOG2E = math.log2(math.e)

V7X_VMEM_BYTES = 64 * 1024 * 1024
VMEM_LIMIT_BYTES = V7X_VMEM_BYTES - 8 * 1024 * 1024
SUBLANES = 8
LANES = 128

ROW_TILE = 512
FF_CHUNK = 1024
ATTN_TILE = 512
CONV_TILE = 512
CONV_HALO = 32
CONV_ROWS = 32
CONV_LANES = 256
NORM_ROWS = 128
HGRN_CHUNK = 128


def _params(*semantics):
    return pltpu.CompilerParams(dimension_semantics=semantics, vmem_limit_bytes=VMEM_LIMIT_BYTES)


def _resident(shape):
    return pl.BlockSpec(shape, lambda *_: (0,) * len(shape), pipeline_mode=pl.Buffered(1))


def _rms_norm(x, g):
    ms = jnp.mean(x * x, axis=-1, keepdims=True)
    return x * lax.rsqrt(ms + NORM_EPS) * g


def _dot(a, b):
    return jnp.dot(a, b, preferred_element_type=F32)


def _dot_nt(a, b):
    return lax.dot_general(a, b, (((1,), (1,)), ((), ())), preferred_element_type=F32)


def _norm_proj_kernel(x_ref, g_ref, w_ref, o_ref):
    hn = _rms_norm(x_ref[...], g_ref[...]).astype(BF16)
    n_out = o_ref.shape[1]
    chunk = min(n_out, FF_CHUNK)
    for c in range(0, n_out, chunk):
        width = min(chunk, n_out - c)
        o_ref[:, c:c + width] = _dot(hn, w_ref[:, c:c + width]).astype(o_ref.dtype)


def _norm_glu_kernel(x_ref, g_ref, w_ref, b_ref, o_ref):
    hn = _rms_norm(x_ref[...], g_ref[...]).astype(BF16)
    a = _dot(hn, w_ref[:, :D_MODEL]) + b_ref[:, :D_MODEL]
    gate = _dot(hn, w_ref[:, D_MODEL:]) + b_ref[:, D_MODEL:]
    o_ref[...] = a * jax.nn.sigmoid(gate)


def _norm_proj(h, g, w, out_dtype):
    n, d = h.shape
    n_out = w.shape[1]
    return pl.pallas_call(
        _norm_proj_kernel,
        grid=(n // ROW_TILE,),
        in_specs=[pl.BlockSpec((ROW_TILE, d), lambda i: (i, 0)), _resident((1, d)), _resident((d, n_out))],
        out_specs=pl.BlockSpec((ROW_TILE, n_out), lambda i: (i, 0)),
        out_shape=jax.ShapeDtypeStruct((n, n_out), out_dtype),
        compiler_params=_params("parallel"),
        name="norm_proj",
    )(h, g, w)


def _norm_glu(h, g, w, b):
    n, d = h.shape
    return pl.pallas_call(
        _norm_glu_kernel,
        grid=(n // ROW_TILE,),
        in_specs=[pl.BlockSpec((ROW_TILE, d), lambda i: (i, 0)), _resident((1, d)),
                  _resident((d, 2 * d)), _resident((1, 2 * d))],
        out_specs=pl.BlockSpec((ROW_TILE, d), lambda i: (i, 0)),
        out_shape=jax.ShapeDtypeStruct((n, d), F32),
        compiler_params=_params("parallel"),
        name="norm_glu",
    )(h, g, w, b)


def _attn_bias():
    qi = jnp.arange(WINDOW)[:, None]
    col = jnp.arange(4 * WINDOW)[None, :]
    si = (col // (2 * WINDOW)) * WINDOW + col % WINDOW
    second = (col // WINDOW) % 2
    dist = WINDOW + qi - si
    valid = (dist >= 0) & (dist < WINDOW)
    head = 2 * jnp.arange(ATTN_HEADS // 2)[:, None, None] + second[None]
    slopes = jnp.exp2(-8.0 * (head + 1).astype(F32) / ATTN_HEADS)
    return jnp.where(valid[None], -(slopes * dist[None].astype(F32)), MASK_VALUE)


def _attn_kernel(sink_ref, q_ref, kvc_ref, kvp_ref, bias_ref, o_ref):
    first_tile = pl.program_id(1) == 0
    pair = 2 * HEAD_DIM
    nblk = ATTN_TILE // WINDOW
    group = ATTN_HEADS // ATTN_KV_HEADS
    scale = HEAD_DIM ** -0.5
    low = lax.broadcasted_iota(jnp.int32, (WINDOW, pair), 1) < HEAD_DIM
    ones_bd = jnp.concatenate([jnp.where(low, 1.0, 0.0), jnp.where(low, 0.0, 1.0)], axis=0).astype(BF16)
    prev_cols = lax.broadcasted_iota(jnp.int32, (WINDOW, 4 * WINDOW), 1) < 2 * WINDOW

    def expand(x, xr, kvh):
        first, second = (x, xr) if kvh == 0 else (xr, x)
        return jnp.concatenate([jnp.where(low, first, 0.0), jnp.where(low, 0.0, second)], axis=0).astype(BF16)

    kblk, vblk = [], []
    for r in range(nblk + 1):
        kv = (kvp_ref[...] if r == 0 else kvc_ref[(r - 1) * WINDOW:r * WINDOW, :]).astype(F32)
        k2, v2 = kv[:, :pair] * scale, kv[:, pair:]
        k2r, v2r = pltpu.roll(k2, HEAD_DIM, 1), pltpu.roll(v2, HEAD_DIM, 1)
        kblk.append([expand(k2, k2r, kvh) for kvh in range(ATTN_KV_HEADS)])
        vblk.append([jnp.concatenate([expand(v2, v2r, kvh), ones_bd], axis=1) for kvh in range(ATTN_KV_HEADS)])

    for r in range(nblk):
        rows = slice(r * WINDOW, (r + 1) * WINDOW)
        kb = [jnp.concatenate([kblk[r][kvh], kblk[r + 1][kvh]], axis=0) for kvh in range(ATTN_KV_HEADS)]
        vb = [jnp.concatenate([vblk[r][kvh], vblk[r + 1][kvh]], axis=0) for kvh in range(ATTN_KV_HEADS)]
        for j in range(ATTN_HEADS // 2):
            kvh = (2 * j) // group
            cols = slice(j * pair, (j + 1) * pair)
            s = _dot_nt(q_ref[rows, cols], kb[kvh]) + bias_ref[j]
            if r == 0:
                s = jnp.where(first_tile & prev_cols, MASK_VALUE, s)
            sa0, sb0, sa1, sb1 = (s[:, i * WINDOW:(i + 1) * WINDOW] for i in range(4))
            sink_a, sink_b = sink_ref[2 * j], sink_ref[2 * j + 1]
            ma = jnp.maximum(jnp.max(jnp.maximum(sa0, sa1), axis=-1, keepdims=True), sink_a)
            mb = jnp.maximum(jnp.max(jnp.maximum(sb0, sb1), axis=-1, keepdims=True), sink_b)
            e = jnp.concatenate([jnp.exp(sa0 - ma), jnp.exp(sb0 - mb), jnp.exp(sa1 - ma), jnp.exp(sb1 - mb)],
                                axis=1).astype(BF16)
            ov = _dot(e, vb[kvh])
            sink_term = jnp.where(low, jnp.exp(sink_a - ma), jnp.exp(sink_b - mb))
            o_ref[rows, cols] = (ov[:, :pair] / (ov[:, pair:] + sink_term)).astype(o_ref.dtype)


def _attention_core(qkv, sinks, batch, seq):
    n = batch * seq
    nt = seq // ATTN_TILE
    blocks_per_tile = ATTN_TILE // WINDOW
    q_cols = ATTN_HEADS * HEAD_DIM
    kv_cols = 2 * ATTN_KV_HEADS * HEAD_DIM
    kv_col_block = q_cols // kv_cols
    bias = _attn_bias()

    def prev_block(b, i):
        return ((b * nt + i) * blocks_per_tile - jnp.minimum(i, 1), kv_col_block)

    return pl.pallas_call(
        _attn_kernel,
        grid=(batch, nt),
        in_specs=[
            pl.BlockSpec(memory_space=pltpu.SMEM),
            pl.BlockSpec((ATTN_TILE, q_cols), lambda b, i: (b * nt + i, 0)),
            pl.BlockSpec((ATTN_TILE, kv_cols), lambda b, i: (b * nt + i, kv_col_block)),
            pl.BlockSpec((WINDOW, kv_cols), prev_block),
            _resident(bias.shape),
        ],
        out_specs=pl.BlockSpec((ATTN_TILE, q_cols), lambda b, i: (b * nt + i, 0)),
        out_shape=jax.ShapeDtypeStruct((n, q_cols), BF16),
        compiler_params=_params("parallel", "parallel"),
        name="attn_core",
    )(sinks, qkv, qkv, qkv, bias)


def _conv_kernel(u_ref, w_ref, bdw_ref, lng_ref, lnb_ref, o_ref, buf_ref, wb_ref, c_ref):
    t = pl.program_id(1)
    tile = u_ref.shape[0]

    @pl.when((pl.program_id(0) == 0) & (t == 0))
    def _():
        for j in range(CONV_WIDTH):
            wb_ref[j] = jnp.broadcast_to(w_ref[j:j + 1, :], (SUBLANES, D_MODEL))

    @pl.when(t == 0)
    def _():
        buf_ref[0, 0:CONV_HALO, :] = jnp.zeros((CONV_HALO, D_MODEL), F32)

    @pl.when(t > 0)
    def _():
        buf_ref[0, 0:CONV_HALO, :] = buf_ref[0, tile:tile + CONV_HALO, :]

    buf_ref[0, CONV_HALO:, :] = u_ref[...]
    shifted_rows = tile + CONV_HALO - SUBLANES
    for r in range(1, SUBLANES):
        buf_ref[r, 0:shifted_rows, :] = buf_ref[0, r:r + shifted_rows, :]

    first_tap_row = CONV_HALO - (CONV_WIDTH - 1)
    groups = CONV_ROWS // SUBLANES

    def conv_rows(i, carry):
        base = pl.multiple_of(i * CONV_ROWS, CONV_ROWS)
        for l in range(0, D_MODEL, CONV_LANES):
            lanes = slice(l, l + CONV_LANES)
            acc = jnp.broadcast_to(bdw_ref[:, lanes], (groups, SUBLANES, CONV_LANES))
            for r in range(SUBLANES):
                taps = [j for j in range(CONV_WIDTH) if (first_tap_row + j) % SUBLANES == r]
                a0 = (first_tap_row + taps[0]) // SUBLANES
                span = (first_tap_row + taps[-1]) // SUBLANES - a0 + groups
                big = buf_ref[r, pl.ds(base + a0 * SUBLANES, span * SUBLANES), lanes]
                big = big.reshape(span, SUBLANES, CONV_LANES)
                for j in taps:
                    a = (first_tap_row + j) // SUBLANES - a0
                    acc = acc + wb_ref[j, :, lanes] * big[a:a + groups]
            c_ref[pl.ds(base, CONV_ROWS), lanes] = acc.reshape(CONV_ROWS, CONV_LANES)
        return carry

    lax.fori_loop(0, tile // CONV_ROWS, conv_rows, 0)

    def norm_rows(i, carry):
        base = pl.multiple_of(i * NORM_ROWS, NORM_ROWS)
        c = c_ref[pl.ds(base, NORM_ROWS), :]
        mu = jnp.mean(c, axis=-1, keepdims=True)
        cen = c - mu
        var = jnp.mean(cen * cen, axis=-1, keepdims=True)
        y = cen * lax.rsqrt(var + NORM_EPS) * lng_ref[...] + lnb_ref[...]
        o_ref[pl.ds(base, NORM_ROWS), :] = (y * jax.nn.sigmoid(y)).astype(o_ref.dtype)
        return carry

    lax.fori_loop(0, tile // NORM_ROWS, norm_rows, 0)


def _conv_core(u, w_dw, b_dw, ln_g, ln_b, batch, seq):
    n, d = u.shape
    nt = seq // CONV_TILE
    row = lambda b, t: (b * nt + t, 0)
    return pl.pallas_call(
        _conv_kernel,
        grid=(batch, nt),
        in_specs=[pl.BlockSpec((CONV_TILE, d), row), _resident((CONV_WIDTH, d)),
                  _resident((1, d)), _resident((1, d)), _resident((1, d))],
        out_specs=pl.BlockSpec((CONV_TILE, d), row),
        out_shape=jax.ShapeDtypeStruct((n, d), BF16),
        scratch_shapes=[pltpu.VMEM((SUBLANES, CONV_TILE + CONV_HALO, d), F32),
                        pltpu.VMEM((CONV_WIDTH, SUBLANES, d), F32),
                        pltpu.VMEM((CONV_TILE, d), F32)],
        compiler_params=_params("arbitrary", "arbitrary"),
        name="conv_core",
    )(u, w_dw, b_dw, ln_g, ln_b)


def _hgrn_constants():
    c = HGRN_CHUNK
    t = jnp.arange(c)[:, None]
    s = jnp.arange(c)[None, :]
    tri = (s <= t).astype(BF16)
    masks = []
    for hs in (32, 16, 8):
        same = (t // (2 * hs)) == (s // (2 * hs))
        masks.append(same & ((t // hs) % 2 == 1) & ((s // hs) % 2 == 0))
    masks.append(((t // SUBLANES) == (s // SUBLANES)) & (s <= t))
    slot = jnp.arange(SUBLANES * HGRN_DK)[:, None] // HGRN_DK
    fold = (slot == jnp.arange(c)[None, :] % SUBLANES).astype(BF16)
    return tri, jnp.stack(masks).astype(F32), fold


def _hgrn_kernel(x_ref, lbp_ref, ng_ref, tri_ref, mask_ref, fold_ref, o_ref,
                 q_s, k_s, b_s, st_ref, *, layer):
    chunk = HGRN_CHUNK
    blocks = chunk // SUBLANES

    @pl.when(pl.program_id(1) == 0)
    def _():
        st_ref[...] = jnp.zeros(st_ref.shape, F32)

    p = lbp_ref[...]
    e = jnp.exp(p - jnp.max(p, axis=0, keepdims=True))
    sm = e / jnp.sum(e, axis=0, keepdims=True)
    lb = jnp.zeros((1, D_MODEL), F32)
    for l in range(1, layer + 1):
        lb = lb + sm[l:l + 1, :]

    qr = x_ref[:, 0:D_MODEL]
    q_s[...] = qr * jax.nn.sigmoid(qr) * (HGRN_DK ** -0.5)
    f = lb + (1.0 - lb) * jax.nn.sigmoid(x_ref[:, D_MODEL:2 * D_MODEL])
    k_s[...] = 1.0 - f
    logf = jnp.log(f)
    g1 = logf.astype(BF16)
    r1 = logf - g1.astype(F32)
    g2 = r1.astype(BF16)
    g3 = (r1 - g2.astype(F32)).astype(BF16)
    tri = tri_ref[...]
    b_s[...] = (_dot(tri, g1) + _dot(tri, g2) + _dot(tri, g3)) * LOG2E

    for hd in range(HGRN_HEADS):
        sl = slice(hd * HGRN_DK, (hd + 1) * HGRN_DK)
        qh, kh, bh = q_s[:, sl], k_s[:, sl], b_s[:, sl]
        vh = x_ref[:, 2 * D_MODEL + hd * HGRN_DK:2 * D_MODEL + (hd + 1) * HGRN_DK]
        gr = x_ref[:, 3 * D_MODEL + hd * HGRN_DK:3 * D_MODEL + (hd + 1) * HGRN_DK]
        vb = vh.astype(BF16)
        b_last = bh[chunk - 1:chunk, :]
        state = st_ref[hd]

        o = _dot_nt((qh * jnp.exp2(bh)).astype(BF16), state.astype(BF16))

        a = jnp.zeros((chunk, chunk), F32)
        for lvl, hs in enumerate((64, 32, 16, 8)):
            q_parts, k_parts = [], []
            zeros = jnp.zeros((hs, HGRN_DK), F32)
            for s0 in range(0, chunk, 2 * hs):
                pivot = bh[s0 + hs - 1:s0 + hs, :]
                lo, up = slice(s0, s0 + hs), slice(s0 + hs, s0 + 2 * hs)
                q_parts += [zeros, qh[up] * jnp.exp2(bh[up] - pivot)]
                k_parts += [kh[lo] * jnp.exp2(pivot - bh[lo]), zeros]
            al = _dot_nt(jnp.concatenate(q_parts, axis=0).astype(BF16),
                         jnp.concatenate(k_parts, axis=0).astype(BF16))
            a = a + (al if lvl == 0 else al * mask_ref[lvl - 1])

        b3 = bh.reshape(blocks, SUBLANES, HGRN_DK)
        q3 = qh.reshape(blocks, SUBLANES, HGRN_DK)
        k3 = kh.reshape(blocks, SUBLANES, HGRN_DK)
        cols = []
        for r in range(SUBLANES):
            decay = jnp.exp2(jnp.minimum(b3 - b3[:, r:r + 1, :], 0.0))
            cols.append((q3 * decay * k3[:, r:r + 1, :]).reshape(chunk, HGRN_DK).astype(BF16))
        a = a + _dot(jnp.concatenate(cols, axis=1), fold_ref[...]) * mask_ref[3]

        o = o + _dot(a.astype(BF16), vb)

        k_dec = (kh * jnp.exp2(b_last - bh)).astype(BF16)
        st_ref[hd] = state * jnp.exp2(b_last) + _dot(vh.T.astype(BF16), k_dec)

        on = o * lax.rsqrt(jnp.mean(o * o, axis=-1, keepdims=True) + NORM_EPS)
        o_ref[:, sl] = (on * ng_ref[:, sl] * (gr * jax.nn.sigmoid(gr))).astype(o_ref.dtype)


def _hgrn_core(qfig, lower_bounds, norm_g, layer, batch, seq):
    n = qfig.shape[0]
    nc = seq // HGRN_CHUNK
    tri, masks, fold = _hgrn_constants()
    return pl.pallas_call(
        functools.partial(_hgrn_kernel, layer=layer),
        grid=(batch, nc),
        in_specs=[pl.BlockSpec((HGRN_CHUNK, 4 * D_MODEL), lambda b, c: (b * nc + c, 0)),
                  _resident(lower_bounds.shape), _resident((1, D_MODEL)),
                  _resident(tri.shape), _resident(masks.shape), _resident(fold.shape)],
        out_specs=pl.BlockSpec((HGRN_CHUNK, D_MODEL), lambda b, c: (b * nc + c, 0)),
        out_shape=jax.ShapeDtypeStruct((n, D_MODEL), BF16),
        scratch_shapes=[pltpu.VMEM((HGRN_CHUNK, D_MODEL), F32)] * 3
                       + [pltpu.VMEM((HGRN_HEADS, HGRN_DK, HGRN_DK), F32)],
        compiler_params=_params("arbitrary", "arbitrary"),
        name="hgrn_core",
    )(qfig, lower_bounds, norm_g, tri, masks, fold)


def _resid_mlp_kernel(h_ref, y_ref, wo_ref, bo_ref, g_ref, w1_ref, w2_ref, gf_ref, o_ref, *,
                      has_bias, final_norm):
    h2 = h_ref[...] + _dot(y_ref[...], wo_ref[...])
    if has_bias:
        h2 = h2 + bo_ref[...]
    hn = _rms_norm(h2, g_ref[...]).astype(BF16)
    acc = h2
    for c in range(0, D_FF, FF_CHUNK):
        up = jnp.maximum(_dot(hn, w1_ref[:, c:c + FF_CHUNK]), 0.0)
        acc = acc + _dot((up * up).astype(BF16), w2_ref[c:c + FF_CHUNK, :])
    if final_norm:
        acc = _rms_norm(acc, gf_ref[...])
    o_ref[...] = acc


def _resid_mlp(h, y, wo, bo, g, w1, w2, gf, has_bias, final_norm):
    n, d = h.shape
    row = lambda i: (i, 0)
    return pl.pallas_call(
        functools.partial(_resid_mlp_kernel, has_bias=has_bias, final_norm=final_norm),
        grid=(n // ROW_TILE,),
        in_specs=[pl.BlockSpec((ROW_TILE, d), row), pl.BlockSpec((ROW_TILE, d), row),
                  _resident((d, d)), _resident((1, d)), _resident((1, d)),
                  _resident((d, D_FF)), _resident((D_FF, d)), _resident((1, d))],
        out_specs=pl.BlockSpec((ROW_TILE, d), row),
        out_shape=jax.ShapeDtypeStruct((n, d), F32),
        compiler_params=_params("parallel"),
        name="resid_mlp",
    )(h, y, wo, bo, g, w1, w2, gf)


def kernel(x, attn_w_qkv, attn_sinks, attn_w_o, conv_w_pw1, conv_b_pw1, conv_w_dw, conv_b_dw, conv_ln_g,
           conv_ln_b, conv_w_pw2, conv_b_pw2, hgrn_w_qfig, hgrn_lower_bounds, hgrn_norm_g, hgrn_w_o,
           norm_mixer, norm_mlp, mlp_w1, mlp_w2, final_norm):
    batch, seq, d = x.shape
    h = x.reshape(batch * seq, d)
    row = lambda v: v.reshape(1, -1)
    zero_bias = jnp.zeros((1, d), F32)
    for i in range(DEPTH):
        kind, j = i % N_MIXERS, i // N_MIXERS
        g_mix = row(norm_mixer[i])
        bias, has_bias = zero_bias, False
        if kind == 0:
            qkv = _norm_proj(h, g_mix, attn_w_qkv[j].astype(BF16), BF16)
            y = _attention_core(qkv, attn_sinks[j], batch, seq)
            wo = attn_w_o[j]
        elif kind == 1:
            u = _norm_glu(h, g_mix, conv_w_pw1[j].astype(BF16), row(conv_b_pw1[j]))
            y = _conv_core(u, conv_w_dw[j], row(conv_b_dw[j]), row(conv_ln_g[j]), row(conv_ln_b[j]), batch, seq)
            wo, bias, has_bias = conv_w_pw2[j], row(conv_b_pw2[j]), True
        else:
            qfig = _norm_proj(h, g_mix, hgrn_w_qfig[j].astype(BF16), F32)
            y = _hgrn_core(qfig, hgrn_lower_bounds, row(hgrn_norm_g[j]), i, batch, seq)
            wo = hgrn_w_o[j]
        h = _resid_mlp(h, y, wo.astype(BF16), bias, row(norm_mlp[i]), mlp_w1[i].astype(BF16),
                       mlp_w2[i].astype(BF16), row(final_norm), has_bias, i == DEPTH - 1)
    return h.reshape(batch, seq, d)
```

```python
import functools
import math

import jax
import jax.numpy as jnp
from jax import lax
from jax.experimental import pallas as pl
from jax.experimental.pallas import tpu as pltpu

F32 = jnp.float32
BF16 = jnp.bfloat16

D_MODEL = 1024
DEPTH = 4
N_MIXERS = 3
ATTN_HEADS = 16
ATTN_KV_HEADS = 2
HEAD_DIM = 64
WINDOW = 128
QKV_DIM = (ATTN_HEADS + 2 * ATTN_KV_HEADS) * HEAD_DIM
CONV_WIDTH = 31
HGRN_HEADS = 8
HGRN_DK = 128
D_FF = 4 * D_MODEL
NORM_EPS = 1e-6
MASK_VALUE = -1e30
LOG2E = math.log2(math.e)

V7X_VMEM_BYTES = 64 * 1024 * 1024
VMEM_LIMIT_BYTES = V7X_VMEM_BYTES - 8 * 1024 * 1024
SUBLANES = 8
LANES = 128

ROW_TILE = 1024
FF_CHUNK = 1024
ATTN_TILE = 512
CONV_TILE = 512
CONV_HALO = 32
CONV_ROWS = 32
CONV_LANES = 256
NORM_ROWS = 128
HGRN_CHUNK = 128
HGRN_LEVELS = (64, 32, 16, 8, 4, 2, 1)
HEAD_SKEW = 2


def _params(*semantics):
    return pltpu.CompilerParams(dimension_semantics=semantics, vmem_limit_bytes=VMEM_LIMIT_BYTES)


def _resident(shape, lead=None):
    if lead is None:
        return pl.BlockSpec(shape, lambda *_: (0,) * len(shape), pipeline_mode=pl.Buffered(1))
    return pl.BlockSpec((None,) + tuple(shape[1:]), lambda *_: (lead,) + (0,) * (len(shape) - 1),
                        pipeline_mode=pl.Buffered(1))


def _rms_norm(x, g):
    ms = jnp.mean(x * x, axis=-1, keepdims=True)
    return x * lax.rsqrt(ms + NORM_EPS) * g


def _dot(a, b):
    return jnp.dot(a, b, preferred_element_type=F32)


def _dot_nt(a, b):
    return lax.dot_general(a, b, (((1,), (1,)), ((), ())), preferred_element_type=F32)


def _norm_proj_kernel(x_ref, g_ref, w_ref, o_ref):
    hn = _rms_norm(x_ref[...], g_ref[...]).astype(BF16)
    n_out = o_ref.shape[1]
    chunk = min(n_out, FF_CHUNK)
    for c in range(0, n_out, chunk):
        width = min(chunk, n_out - c)
        o_ref[:, c:c + width] = _dot(hn, w_ref[:, c:c + width]).astype(o_ref.dtype)


def _norm_glu_kernel(x_ref, g_ref, w_ref, b_ref, o_ref):
    hn = _rms_norm(x_ref[...], g_ref[...]).astype(BF16)
    a = _dot(hn, w_ref[:, :D_MODEL]) + b_ref[:, :D_MODEL]
    gate = _dot(hn, w_ref[:, D_MODEL:]) + b_ref[:, D_MODEL:]
    o_ref[...] = a * jax.nn.sigmoid(gate)


def _norm_proj(h, g, w_stack, layer, out_dtype):
    n, d = h.shape
    n_out = w_stack.shape[2]
    return pl.pallas_call(
        _norm_proj_kernel,
        grid=(n // ROW_TILE,),
        in_specs=[pl.BlockSpec((ROW_TILE, d), lambda i: (i, 0)), _resident((1, d)),
                  _resident(w_stack.shape, layer)],
        out_specs=pl.BlockSpec((ROW_TILE, n_out), lambda i: (i, 0)),
        out_shape=jax.ShapeDtypeStruct((n, n_out), out_dtype),
        compiler_params=_params("parallel"),
        name="norm_proj",
    )(h, g, w_stack)


def _norm_glu(h, g, w_stack, layer, b):
    n, d = h.shape
    return pl.pallas_call(
        _norm_glu_kernel,
        grid=(n // ROW_TILE,),
        in_specs=[pl.BlockSpec((ROW_TILE, d), lambda i: (i, 0)), _resident((1, d)),
                  _resident(w_stack.shape, layer), _resident((1, 2 * d))],
        out_specs=pl.BlockSpec((ROW_TILE, d), lambda i: (i, 0)),
        out_shape=jax.ShapeDtypeStruct((n, d), F32),
        compiler_params=_params("parallel"),
        name="norm_glu",
    )(h, g, w_stack, b)


def _attn_bias():
    qi = jnp.arange(WINDOW)[:, None]
    col = jnp.arange(4 * WINDOW)[None, :]
    si = (col // (2 * WINDOW)) * WINDOW + col % WINDOW
    second = (col // WINDOW) % 2
    dist = WINDOW + qi - si
    valid = (dist >= 0) & (dist < WINDOW)
    head = 2 * jnp.arange(ATTN_HEADS // 2)[:, None, None] + second[None]
    slopes = jnp.exp2(-8.0 * (head + 1).astype(F32) / ATTN_HEADS)
    return jnp.where(valid[None], -(slopes * dist[None].astype(F32)), MASK_VALUE)


def _attn_kernel(sink_ref, q_ref, kvc_ref, kvp_ref, bias_ref, o_ref):
    first_tile = pl.program_id(1) == 0
    pair = 2 * HEAD_DIM
    nblk = ATTN_TILE // WINDOW
    group = ATTN_HEADS // ATTN_KV_HEADS
    scale = HEAD_DIM ** -0.5
    low = lax.broadcasted_iota(jnp.int32, (WINDOW, pair), 1) < HEAD_DIM
    ones_bd = jnp.concatenate([jnp.where(low, 1.0, 0.0), jnp.where(low, 0.0, 1.0)], axis=0).astype(BF16)
    prev_cols = lax.broadcasted_iota(jnp.int32, (WINDOW, 4 * WINDOW), 1) < 2 * WINDOW

    def expand(x, xr, kvh):
        first, second = (x, xr) if kvh == 0 else (xr, x)
        return jnp.concatenate([jnp.where(low, first, 0.0), jnp.where(low, 0.0, second)], axis=0).astype(BF16)

    kblk, vblk = [], []
    for r in range(nblk + 1):
        kv = (kvp_ref[...] if r == 0 else kvc_ref[(r - 1) * WINDOW:r * WINDOW, :]).astype(F32)
        k2, v2 = kv[:, :pair] * scale, kv[:, pair:]
        k2r, v2r = pltpu.roll(k2, HEAD_DIM, 1), pltpu.roll(v2, HEAD_DIM, 1)
        kblk.append([expand(k2, k2r, kvh) for kvh in range(ATTN_KV_HEADS)])
        vblk.append([jnp.concatenate([expand(v2, v2r, kvh), ones_bd], axis=1) for kvh in range(ATTN_KV_HEADS)])

    kb = [[jnp.concatenate([kblk[r][kvh], kblk[r + 1][kvh]], axis=0) for kvh in range(ATTN_KV_HEADS)]
          for r in range(nblk)]
    vb = [[jnp.concatenate([vblk[r][kvh], vblk[r + 1][kvh]], axis=0) for kvh in range(ATTN_KV_HEADS)]
          for r in range(nblk)]

    def scores(r, j):
        rows, cols = slice(r * WINDOW, (r + 1) * WINDOW), slice(j * pair, (j + 1) * pair)
        s = _dot_nt(q_ref[rows, cols], kb[r][(2 * j) // group]) + bias_ref[j]
        if r == 0:
            s = jnp.where(first_tile & prev_cols, MASK_VALUE, s)
        return s

    def weigh(r, j, s):
        rows, cols = slice(r * WINDOW, (r + 1) * WINDOW), slice(j * pair, (j + 1) * pair)
        sa0, sb0, sa1, sb1 = (s[:, i * WINDOW:(i + 1) * WINDOW] for i in range(4))
        sink_a, sink_b = sink_ref[2 * j], sink_ref[2 * j + 1]
        ma = jnp.maximum(jnp.max(jnp.maximum(sa0, sa1), axis=-1, keepdims=True), sink_a)
        mb = jnp.maximum(jnp.max(jnp.maximum(sb0, sb1), axis=-1, keepdims=True), sink_b)
        e = jnp.concatenate([jnp.exp(sa0 - ma), jnp.exp(sb0 - mb), jnp.exp(sa1 - ma), jnp.exp(sb1 - mb)],
                            axis=1).astype(BF16)
        ov = _dot(e, vb[r][(2 * j) // group])
        sink_term = jnp.where(low, jnp.exp(sink_a - ma), jnp.exp(sink_b - mb))
        o_ref[rows, cols] = (ov[:, :pair] / (ov[:, pair:] + sink_term)).astype(o_ref.dtype)

    for r in range(nblk):
        for j in range(ATTN_HEADS // 2):
            weigh(r, j, scores(r, j))


def _attention_core(qkv, sinks, batch, seq):
    n = batch * seq
    nt = seq // ATTN_TILE
    blocks_per_tile = ATTN_TILE // WINDOW
    q_cols = ATTN_HEADS * HEAD_DIM
    kv_cols = 2 * ATTN_KV_HEADS * HEAD_DIM
    kv_col_block = q_cols // kv_cols
    bias = _attn_bias()

    def prev_block(b, i):
        return ((b * nt + i) * blocks_per_tile - jnp.minimum(i, 1), kv_col_block)

    return pl.pallas_call(
        _attn_kernel,
        grid=(batch, nt),
        in_specs=[
            pl.BlockSpec(memory_space=pltpu.SMEM),
            pl.BlockSpec((ATTN_TILE, q_cols), lambda b, i: (b * nt + i, 0)),
            pl.BlockSpec((ATTN_TILE, kv_cols), lambda b, i: (b * nt + i, kv_col_block)),
            pl.BlockSpec((WINDOW, kv_cols), prev_block),
            _resident(bias.shape),
        ],
        out_specs=pl.BlockSpec((ATTN_TILE, q_cols), lambda b, i: (b * nt + i, 0)),
        out_shape=jax.ShapeDtypeStruct((n, q_cols), BF16),
        compiler_params=_params("parallel", "parallel"),
        name="attn_core",
    )(sinks, qkv, qkv, qkv, bias)


def _conv_kernel(u_ref, w_ref, bdw_ref, lng_ref, lnb_ref, o_ref, buf_ref, wb_ref, c_ref):
    t = pl.program_id(1)
    tile = u_ref.shape[0]

    @pl.when((pl.program_id(0) == 0) & (t == 0))
    def _():
        for j in range(CONV_WIDTH):
            wb_ref[j] = jnp.broadcast_to(w_ref[j:j + 1, :], (SUBLANES, D_MODEL))

    @pl.when(t == 0)
    def _():
        buf_ref[0, 0:CONV_HALO, :] = jnp.zeros((CONV_HALO, D_MODEL), F32)

    @pl.when(t > 0)
    def _():
        buf_ref[0, 0:CONV_HALO, :] = buf_ref[0, tile:tile + CONV_HALO, :]

    buf_ref[0, CONV_HALO:, :] = u_ref[...]
    shifted_rows = tile + CONV_HALO - SUBLANES
    for r in range(1, SUBLANES):
        buf_ref[r, 0:shifted_rows, :] = buf_ref[0, r:r + shifted_rows, :]

    first_tap_row = CONV_HALO - (CONV_WIDTH - 1)
    groups = CONV_ROWS // SUBLANES

    def conv_rows(i, carry):
        base = pl.multiple_of(i * CONV_ROWS, CONV_ROWS)
        for l in range(0, D_MODEL, CONV_LANES):
            lanes = slice(l, l + CONV_LANES)
            acc = jnp.broadcast_to(bdw_ref[:, lanes], (groups, SUBLANES, CONV_LANES))
            for r in range(SUBLANES):
                taps = [j for j in range(CONV_WIDTH) if (first_tap_row + j) % SUBLANES == r]
                a0 = (first_tap_row + taps[0]) // SUBLANES
                span = (first_tap_row + taps[-1]) // SUBLANES - a0 + groups
                big = buf_ref[r, pl.ds(base + a0 * SUBLANES, span * SUBLANES), lanes]
                big = big.reshape(span, SUBLANES, CONV_LANES)
                for j in taps:
                    a = (first_tap_row + j) // SUBLANES - a0
                    acc = acc + wb_ref[j, :, lanes] * big[a:a + groups]
            c_ref[pl.ds(base, CONV_ROWS), lanes] = acc.reshape(CONV_ROWS, CONV_LANES)
        return carry

    lax.fori_loop(0, tile // CONV_ROWS, conv_rows, 0)

    def norm_rows(i, carry):
        base = pl.multiple_of(i * NORM_ROWS, NORM_ROWS)
        c = c_ref[pl.ds(base, NORM_ROWS), :]
        mu = jnp.mean(c, axis=-1, keepdims=True)
        cen = c - mu
        var = jnp.mean(cen * cen, axis=-1, keepdims=True)
        y = cen * lax.rsqrt(var + NORM_EPS) * lng_ref[...] + lnb_ref[...]
        o_ref[pl.ds(base, NORM_ROWS), :] = (y * jax.nn.sigmoid(y)).astype(o_ref.dtype)
        return carry

    lax.fori_loop(0, tile // NORM_ROWS, norm_rows, 0)


def _conv_core(u, w_dw, b_dw, ln_g, ln_b, batch, seq):
    n, d = u.shape
    nt = seq // CONV_TILE
    row = lambda b, t: (b * nt + t, 0)
    return pl.pallas_call(
        _conv_kernel,
        grid=(batch, nt),
        in_specs=[pl.BlockSpec((CONV_TILE, d), row), _resident((CONV_WIDTH, d)),
                  _resident((1, d)), _resident((1, d)), _resident((1, d))],
        out_specs=pl.BlockSpec((CONV_TILE, d), row),
        out_shape=jax.ShapeDtypeStruct((n, d), BF16),
        scratch_shapes=[pltpu.VMEM((SUBLANES, CONV_TILE + CONV_HALO, d), F32),
                        pltpu.VMEM((CONV_WIDTH, SUBLANES, d), F32),
                        pltpu.VMEM((CONV_TILE, d), F32)],
        compiler_params=_params("arbitrary", "arbitrary"),
        name="conv_core",
    )(u, w_dw, b_dw, ln_g, ln_b)


def _hgrn_constants():
    c = HGRN_CHUNK
    t = jnp.arange(c)[:, None]
    s = jnp.arange(c)[None, :]
    tri = (s <= t).astype(BF16)
    masks = []
    for hs in HGRN_LEVELS[1:]:
        same = (t // (2 * hs)) == (s // (2 * hs))
        masks.append(same & ((t // hs) % 2 == 1) & ((s // hs) % 2 == 0))
    masks.append(t == s)
    return tri, jnp.stack(masks).astype(F32)


def _hgrn_kernel(x_ref, lbp_ref, ng_ref, tri_ref, mask_ref, o_ref, q_s, k_s, b_s, st_ref, *, layer):
    chunk = HGRN_CHUNK
    blocks = chunk // SUBLANES

    @pl.when(pl.program_id(1) == 0)
    def _():
        st_ref[...] = jnp.zeros(st_ref.shape, F32)

    p = lbp_ref[...]
    e = jnp.exp(p - jnp.max(p, axis=0, keepdims=True))
    sm = e / jnp.sum(e, axis=0, keepdims=True)
    lb = jnp.zeros((1, D_MODEL), F32)
    for l in range(1, layer + 1):
        lb = lb + sm[l:l + 1, :]

    qr = x_ref[:, 0:D_MODEL]
    q_s[...] = qr * jax.nn.sigmoid(qr) * (HGRN_DK ** -0.5)
    f = lb + (1.0 - lb) * jax.nn.sigmoid(x_ref[:, D_MODEL:2 * D_MODEL])
    k_s[...] = 1.0 - f
    logf = jnp.log(f)
    g1 = logf.astype(BF16)
    r1 = logf - g1.astype(F32)
    g2 = r1.astype(BF16)
    g3 = (r1 - g2.astype(F32)).astype(BF16)
    tri = tri_ref[...]
    b_s[...] = (_dot(tri, g1) + _dot(tri, g2) + _dot(tri, g3)) * LOG2E

    row_in_blk = lax.broadcasted_iota(jnp.int32, (blocks, SUBLANES, HGRN_DK), 1)
    upper_of_8 = row_in_blk >= 4
    sign4 = jnp.where(upper_of_8, 1.0, -1.0)
    sign2 = jnp.where(row_in_blk % 4 >= 2, 1.0, -1.0)

    def scores(hd):
        sl = slice(hd * HGRN_DK, (hd + 1) * HGRN_DK)
        qh, kh, bh = q_s[:, sl], k_s[:, sl], b_s[:, sl]
        kb = kh.astype(BF16)
        state = st_ref[hd]

        o = _dot_nt((qh * jnp.exp2(bh)).astype(BF16), state.astype(BF16))

        a = jnp.zeros((chunk, chunk), F32)
        for lvl, hs in enumerate(HGRN_LEVELS[:4]):
            q_parts, k_parts = [], []
            zeros = jnp.zeros((hs, HGRN_DK), F32)
            for s0 in range(0, chunk, 2 * hs):
                pivot = bh[s0 + hs - 1:s0 + hs, :]
                lo, up = slice(s0, s0 + hs), slice(s0 + hs, s0 + 2 * hs)
                q_parts += [zeros, qh[up] * jnp.exp2(bh[up] - pivot)]
                k_parts += [kh[lo] * jnp.exp2(pivot - bh[lo]), zeros]
            al = _dot_nt(jnp.concatenate(q_parts, axis=0).astype(BF16),
                         jnp.concatenate(k_parts, axis=0).astype(BF16))
            a = a + (al if lvl == 0 else al * mask_ref[lvl - 1])

        b3 = bh.reshape(blocks, SUBLANES, HGRN_DK)
        q3 = qh.reshape(blocks, SUBLANES, HGRN_DK)
        k3 = kh.reshape(blocks, SUBLANES, HGRN_DK)
        pivot4 = b3[:, 3:4, :]
        pivot2 = jnp.where(upper_of_8, b3[:, 5:6, :], b3[:, 1:2, :])
        for lvl, w3 in ((4, jnp.exp2((b3 - pivot4) * sign4)), (5, jnp.exp2((b3 - pivot2) * sign2))):
            al = _dot_nt((q3 * w3).reshape(chunk, HGRN_DK).astype(BF16),
                         (k3 * w3).reshape(chunk, HGRN_DK).astype(BF16))
            a = a + al * mask_ref[lvl - 1]
        a = a + _dot_nt((qh * (1.0 - kh)).astype(BF16), kb) * mask_ref[5]
        a = a + _dot_nt(qh.astype(BF16), kb) * mask_ref[6]
        return o, a, state

    def mix(hd, o, a, state):
        sl = slice(hd * HGRN_DK, (hd + 1) * HGRN_DK)
        kh, bh = k_s[:, sl], b_s[:, sl]
        vh = x_ref[:, 2 * D_MODEL + hd * HGRN_DK:2 * D_MODEL + (hd + 1) * HGRN_DK]
        b_last = bh[chunk - 1:chunk, :]
        o = o + _dot(a.astype(BF16), vh.astype(BF16))
        k_dec = (kh * jnp.exp2(b_last - bh)).astype(BF16)
        st_ref[hd] = state * jnp.exp2(b_last) + _dot(vh.T.astype(BF16), k_dec)
        return o

    def readout(hd, o):
        sl = slice(hd * HGRN_DK, (hd + 1) * HGRN_DK)
        gr = x_ref[:, 3 * D_MODEL + hd * HGRN_DK:3 * D_MODEL + (hd + 1) * HGRN_DK]
        on = o * lax.rsqrt(jnp.mean(o * o, axis=-1, keepdims=True) + NORM_EPS)
        o_ref[:, sl] = (on * ng_ref[:, sl] * (gr * jax.nn.sigmoid(gr))).astype(o_ref.dtype)

    stage1, stage2 = {}, {}
    for step in range(HGRN_HEADS + 2 * HEAD_SKEW):
        if step < HGRN_HEADS:
            stage1[step] = scores(step)
        if 0 <= step - HEAD_SKEW < HGRN_HEADS:
            stage2[step - HEAD_SKEW] = mix(step - HEAD_SKEW, *stage1.pop(step - HEAD_SKEW))
        if 0 <= step - 2 * HEAD_SKEW < HGRN_HEADS:
            readout(step - 2 * HEAD_SKEW, stage2.pop(step - 2 * HEAD_SKEW))


def _hgrn_core(qfig, lower_bounds, norm_g, layer, batch, seq):
    n = qfig.shape[0]
    nc = seq // HGRN_CHUNK
    tri, masks = _hgrn_constants()
    return pl.pallas_call(
        functools.partial(_hgrn_kernel, layer=layer),
        grid=(batch, nc),
        in_specs=[pl.BlockSpec((HGRN_CHUNK, 4 * D_MODEL), lambda b, c: (b * nc + c, 0)),
                  _resident(lower_bounds.shape), _resident((1, D_MODEL)),
                  _resident(tri.shape), _resident(masks.shape)],
        out_specs=pl.BlockSpec((HGRN_CHUNK, D_MODEL), lambda b, c: (b * nc + c, 0)),
        out_shape=jax.ShapeDtypeStruct((n, D_MODEL), BF16),
        scratch_shapes=[pltpu.VMEM((HGRN_CHUNK, D_MODEL), F32)] * 3
                       + [pltpu.VMEM((HGRN_HEADS, HGRN_DK, HGRN_DK), F32)],
        compiler_params=_params("arbitrary", "arbitrary"),
        name="hgrn_core",
    )(qfig, lower_bounds, norm_g, tri, masks)


def _resid_mlp_kernel(h_ref, y_ref, wo_ref, bo_ref, g_ref, w1_ref, w2_ref, gf_ref, o_ref, *,
                      has_bias, final_norm):
    h2 = h_ref[...] + _dot(y_ref[...], wo_ref[...])
    if has_bias:
        h2 = h2 + bo_ref[...]
    hn = _rms_norm(h2, g_ref[...]).astype(BF16)
    acc = h2
    for c in range(0, D_FF, FF_CHUNK):
        up = jnp.maximum(_dot(hn, w1_ref[:, c:c + FF_CHUNK]), 0.0)
        acc = acc + _dot((up * up).astype(BF16), w2_ref[c:c + FF_CHUNK, :])
    if final_norm:
        acc = _rms_norm(acc, gf_ref[...])
    o_ref[...] = acc


def _resid_mlp(h, y, wo_stack, wo_layer, bo, g, w1_stack, w2_stack, layer, gf, has_bias, final_norm):
    n, d = h.shape
    row = lambda i: (i, 0)
    return pl.pallas_call(
        functools.partial(_resid_mlp_kernel, has_bias=has_bias, final_norm=final_norm),
        grid=(n // ROW_TILE,),
        in_specs=[pl.BlockSpec((ROW_TILE, d), row), pl.BlockSpec((ROW_TILE, d), row),
                  _resident(wo_stack.shape, wo_layer), _resident((1, d)), _resident((1, d)),
                  _resident(w1_stack.shape, layer), _resident(w2_stack.shape, layer), _resident((1, d))],
        out_specs=pl.BlockSpec((ROW_TILE, d), row),
        out_shape=jax.ShapeDtypeStruct((n, d), F32),
        compiler_params=_params("parallel"),
        name="resid_mlp",
    )(h, y, wo_stack, bo, g, w1_stack, w2_stack, gf)


def kernel(x, attn_w_qkv, attn_sinks, attn_w_o, conv_w_pw1, conv_b_pw1, conv_w_dw, conv_b_dw, conv_ln_g,
           conv_ln_b, conv_w_pw2, conv_b_pw2, hgrn_w_qfig, hgrn_lower_bounds, hgrn_norm_g, hgrn_w_o,
           norm_mixer, norm_mlp, mlp_w1, mlp_w2, final_norm):
    batch, seq, d = x.shape
    h = x.reshape(batch * seq, d)
    row = lambda v: v.reshape(1, -1)
    zero_bias = jnp.zeros((1, d), F32)
    w_qkv, w_ao = attn_w_qkv.astype(BF16), attn_w_o.astype(BF16)
    w_pw1, w_pw2 = conv_w_pw1.astype(BF16), conv_w_pw2.astype(BF16)
    w_qfig, w_ho = hgrn_w_qfig.astype(BF16), hgrn_w_o.astype(BF16)
    w1, w2 = mlp_w1.astype(BF16), mlp_w2.astype(BF16)
    for i in range(DEPTH):
        kind, j = i % N_MIXERS, i // N_MIXERS
        g_mix = row(norm_mixer[i])
        bias, has_bias = zero_bias, False
        if kind == 0:
            qkv = _norm_proj(h, g_mix, w_qkv, j, BF16)
            y = _attention_core(qkv, attn_sinks[j], batch, seq)
            wo = w_ao
        elif kind == 1:
            u = _norm_glu(h, g_mix, w_pw1, j, row(conv_b_pw1[j]))
            y = _conv_core(u, conv_w_dw[j], row(conv_b_dw[j]), row(conv_ln_g[j]), row(conv_ln_b[j]), batch, seq)
            wo, bias, has_bias = w_pw2, row(conv_b_pw2[j]), True
        else:
            qfig = _norm_proj(h, g_mix, w_qfig, j, F32)
            y = _hgrn_core(qfig, hgrn_lower_bounds, row(hgrn_norm_g[j]), i, batch, seq)
            wo = w_ho
        h = _resid_mlp(h, y, wo, j, bias, row(norm_mlp[i]), w1, w2, i, row(final_norm),
                       has_bias, i == DEPTH - 1)
    return h.reshape(batch, seq, d)
```

```python
import functools
import math

import jax
import jax.numpy as jnp
from jax import lax
from jax.experimental import pallas as pl
from jax.experimental.pallas import tpu as pltpu

F32 = jnp.float32
BF16 = jnp.bfloat16

D_MODEL = 1024
DEPTH = 4
N_MIXERS = 3
ATTN_HEADS = 16
ATTN_KV_HEADS = 2
HEAD_DIM = 64
WINDOW = 128
QKV_DIM = (ATTN_HEADS + 2 * ATTN_KV_HEADS) * HEAD_DIM
CONV_WIDTH = 31
HGRN_HEADS = 8
HGRN_DK = 128
D_FF = 4 * D_MODEL
NORM_EPS = 1e-6
MASK_VALUE = -1e30
LOG2E = math.log2(math.e)

V7X_VMEM_BYTES = 64 * 1024 * 1024
VMEM_LIMIT_BYTES = V7X_VMEM_BYTES - 8 * 1024 * 1024
SUBLANES = 8
LANES = 128

ROW_TILE = 1024
FF_CHUNK = 1024
ATTN_TILE = 512
CONV_TILE = 512
CONV_HALO = 32
CONV_ROWS = 32
CONV_LANES = 128
NORM_ROWS = 256
HGRN_CHUNK = 128
HGRN_LEVELS = (64, 32, 16, 8, 4, 2, 1)
HEAD_SKEW = 3


def _params(*semantics):
    return pltpu.CompilerParams(dimension_semantics=semantics, vmem_limit_bytes=VMEM_LIMIT_BYTES)


def _resident(shape, lead=None):
    if lead is None:
        return pl.BlockSpec(shape, lambda *_: (0,) * len(shape), pipeline_mode=pl.Buffered(1))
    return pl.BlockSpec((None,) + tuple(shape[1:]), lambda *_: (lead,) + (0,) * (len(shape) - 1),
                        pipeline_mode=pl.Buffered(1))


def _rms_norm(x, g):
    ms = jnp.mean(x * x, axis=-1, keepdims=True)
    return x * lax.rsqrt(ms + NORM_EPS) * g


def _sigmoid(x):
    return 0.5 * jnp.tanh(0.5 * x) + 0.5


def _silu(x):
    h = 0.5 * x
    return h + h * jnp.tanh(h)


def _zero_like(x):
    bits = lax.bitcast_convert_type(x, jnp.uint32)
    sixteen = jnp.uint32(16)
    return lax.bitcast_convert_type((bits >> sixteen) >> sixteen, F32)


def _dot(a, b):
    return jnp.dot(a, b, preferred_element_type=F32)


def _dot_nt(a, b):
    return lax.dot_general(a, b, (((1,), (1,)), ((), ())), preferred_element_type=F32)


def _norm_proj_kernel(x_ref, g_ref, w_ref, o_ref):
    hn = _rms_norm(x_ref[...], g_ref[...]).astype(BF16)
    n_out = o_ref.shape[1]
    chunk = min(n_out, FF_CHUNK)
    for c in range(0, n_out, chunk):
        width = min(chunk, n_out - c)
        o_ref[:, c:c + width] = _dot(hn, w_ref[:, c:c + width]).astype(o_ref.dtype)


def _norm_glu_kernel(x_ref, g_ref, w_ref, b_ref, o_ref):
    hn = _rms_norm(x_ref[...], g_ref[...]).astype(BF16)
    a = _dot(hn, w_ref[:, :D_MODEL]) + b_ref[:, :D_MODEL]
    gate = _dot(hn, w_ref[:, D_MODEL:]) + b_ref[:, D_MODEL:]
    o_ref[...] = a * _sigmoid(gate)


def _norm_proj(h, g, w_stack, layer, out_dtype):
    n, d = h.shape
    n_out = w_stack.shape[2]
    return pl.pallas_call(
        _norm_proj_kernel,
        grid=(n // ROW_TILE,),
        in_specs=[pl.BlockSpec((ROW_TILE, d), lambda i: (i, 0)), _resident((1, d)),
                  _resident(w_stack.shape, layer)],
        out_specs=pl.BlockSpec((ROW_TILE, n_out), lambda i: (i, 0)),
        out_shape=jax.ShapeDtypeStruct((n, n_out), out_dtype),
        compiler_params=_params("parallel"),
        name="norm_proj",
    )(h, g, w_stack)


def _norm_glu(h, g, w_stack, layer, b):
    n, d = h.shape
    return pl.pallas_call(
        _norm_glu_kernel,
        grid=(n // ROW_TILE,),
        in_specs=[pl.BlockSpec((ROW_TILE, d), lambda i: (i, 0)), _resident((1, d)),
                  _resident(w_stack.shape, layer), _resident((1, 2 * d))],
        out_specs=pl.BlockSpec((ROW_TILE, d), lambda i: (i, 0)),
        out_shape=jax.ShapeDtypeStruct((n, d), F32),
        compiler_params=_params("parallel"),
        name="norm_glu",
    )(h, g, w_stack, b)


def _attn_bias():
    qi = jnp.arange(WINDOW)[:, None]
    col = jnp.arange(4 * WINDOW)[None, :]
    si = (col // (2 * WINDOW)) * WINDOW + col % WINDOW
    second = (col // WINDOW) % 2
    dist = WINDOW + qi - si
    valid = (dist >= 0) & (dist < WINDOW)
    head = 2 * jnp.arange(ATTN_HEADS // 2)[:, None, None] + second[None]
    slopes = jnp.exp2(-8.0 * (head + 1).astype(F32) / ATTN_HEADS)
    return jnp.where(valid[None], -(slopes * dist[None].astype(F32)), MASK_VALUE)


def _attn_kernel(sink_ref, q_ref, kvc_ref, kvp_ref, bias_ref, o_ref):
    first_tile = pl.program_id(1) == 0
    pair = 2 * HEAD_DIM
    nblk = ATTN_TILE // WINDOW
    group = ATTN_HEADS // ATTN_KV_HEADS
    scale = HEAD_DIM ** -0.5
    low = lax.broadcasted_iota(jnp.int32, (WINDOW, pair), 1) < HEAD_DIM
    ones_bd = jnp.concatenate([jnp.where(low, 1.0, 0.0), jnp.where(low, 0.0, 1.0)], axis=0).astype(BF16)
    prev_cols = lax.broadcasted_iota(jnp.int32, (WINDOW, 4 * WINDOW), 1) < 2 * WINDOW

    def expand(x, xr, kvh):
        first, second = (x, xr) if kvh == 0 else (xr, x)
        return jnp.concatenate([jnp.where(low, first, 0.0), jnp.where(low, 0.0, second)], axis=0).astype(BF16)

    kblk, vblk = [], []
    for r in range(nblk + 1):
        kv = (kvp_ref[...] if r == 0 else kvc_ref[(r - 1) * WINDOW:r * WINDOW, :]).astype(F32)
        k2, v2 = kv[:, :pair] * scale, kv[:, pair:]
        k2r, v2r = pltpu.roll(k2, HEAD_DIM, 1), pltpu.roll(v2, HEAD_DIM, 1)
        kblk.append([expand(k2, k2r, kvh) for kvh in range(ATTN_KV_HEADS)])
        vblk.append([jnp.concatenate([expand(v2, v2r, kvh), ones_bd], axis=1) for kvh in range(ATTN_KV_HEADS)])

    kb = [[jnp.concatenate([kblk[r][kvh], kblk[r + 1][kvh]], axis=0) for kvh in range(ATTN_KV_HEADS)]
          for r in range(nblk)]
    vb = [[jnp.concatenate([vblk[r][kvh], vblk[r + 1][kvh]], axis=0) for kvh in range(ATTN_KV_HEADS)]
          for r in range(nblk)]

    def scores(r, j):
        rows, cols = slice(r * WINDOW, (r + 1) * WINDOW), slice(j * pair, (j + 1) * pair)
        s = _dot_nt(q_ref[rows, cols], kb[r][(2 * j) // group]) + bias_ref[j]
        if r == 0:
            s = jnp.where(first_tile & prev_cols, MASK_VALUE, s)
        return s

    def weigh(r, j, s):
        rows, cols = slice(r * WINDOW, (r + 1) * WINDOW), slice(j * pair, (j + 1) * pair)
        sa0, sb0, sa1, sb1 = (s[:, i * WINDOW:(i + 1) * WINDOW] for i in range(4))
        sink_a, sink_b = sink_ref[2 * j], sink_ref[2 * j + 1]
        ma = jnp.maximum(jnp.max(jnp.maximum(sa0, sa1), axis=-1, keepdims=True), sink_a)
        mb = jnp.maximum(jnp.max(jnp.maximum(sb0, sb1), axis=-1, keepdims=True), sink_b)
        e = jnp.concatenate([jnp.exp(sa0 - ma), jnp.exp(sb0 - mb), jnp.exp(sa1 - ma), jnp.exp(sb1 - mb)],
                            axis=1).astype(BF16)
        ov = _dot(e, vb[r][(2 * j) // group])
        sink_term = jnp.where(low, jnp.exp(sink_a - ma), jnp.exp(sink_b - mb))
        o_ref[rows, cols] = (ov[:, :pair] / (ov[:, pair:] + sink_term)).astype(o_ref.dtype)

    for r in range(nblk):
        for j in range(ATTN_HEADS // 2):
            weigh(r, j, scores(r, j))


def _attention_core(qkv, sinks, batch, seq):
    n = batch * seq
    nt = seq // ATTN_TILE
    blocks_per_tile = ATTN_TILE // WINDOW
    q_cols = ATTN_HEADS * HEAD_DIM
    kv_cols = 2 * ATTN_KV_HEADS * HEAD_DIM
    kv_col_block = q_cols // kv_cols
    bias = _attn_bias()

    def prev_block(b, i):
        return ((b * nt + i) * blocks_per_tile - jnp.minimum(i, 1), kv_col_block)

    return pl.pallas_call(
        _attn_kernel,
        grid=(batch, nt),
        in_specs=[
            pl.BlockSpec(memory_space=pltpu.SMEM),
            pl.BlockSpec((ATTN_TILE, q_cols), lambda b, i: (b * nt + i, 0)),
            pl.BlockSpec((ATTN_TILE, kv_cols), lambda b, i: (b * nt + i, kv_col_block)),
            pl.BlockSpec((WINDOW, kv_cols), prev_block),
            _resident(bias.shape),
        ],
        out_specs=pl.BlockSpec((ATTN_TILE, q_cols), lambda b, i: (b * nt + i, 0)),
        out_shape=jax.ShapeDtypeStruct((n, q_cols), BF16),
        compiler_params=_params("parallel", "parallel"),
        name="attn_core",
    )(sinks, qkv, qkv, qkv, bias)


def _conv_kernel(u_ref, w_ref, bdw_ref, lng_ref, lnb_ref, o_ref, buf_ref, wb_ref, c_ref):
    t = pl.program_id(1)
    tile = u_ref.shape[0]

    @pl.when((pl.program_id(0) == 0) & (t == 0))
    def _():
        for j in range(CONV_WIDTH):
            wb_ref[j] = jnp.broadcast_to(w_ref[j:j + 1, :], (SUBLANES, D_MODEL))

    @pl.when(t == 0)
    def _():
        buf_ref[0, 0:CONV_HALO, :] = jnp.zeros((CONV_HALO, D_MODEL), F32)

    @pl.when(t > 0)
    def _():
        buf_ref[0, 0:CONV_HALO, :] = buf_ref[0, tile:tile + CONV_HALO, :]

    buf_ref[0, CONV_HALO:, :] = u_ref[...]
    shifted_rows = tile + CONV_HALO - SUBLANES
    for r in range(1, SUBLANES):
        buf_ref[r, 0:shifted_rows, :] = buf_ref[0, r:r + shifted_rows, :]

    first_tap_row = CONV_HALO - (CONV_WIDTH - 1)
    groups = CONV_ROWS // SUBLANES

    def conv_rows(i, carry):
        base = pl.multiple_of(i * CONV_ROWS, CONV_ROWS)
        acc = None
        for l in range(0, D_MODEL, CONV_LANES):
            lanes = slice(l, l + CONV_LANES)
            start = bdw_ref[:, lanes]
            if acc is not None:
                start = start + _zero_like(acc[0])
            acc = jnp.broadcast_to(start, (groups, SUBLANES, CONV_LANES))
            for r in range(SUBLANES):
                taps = [j for j in range(CONV_WIDTH) if (first_tap_row + j) % SUBLANES == r]
                a0 = (first_tap_row + taps[0]) // SUBLANES
                span = (first_tap_row + taps[-1]) // SUBLANES - a0 + groups
                big = buf_ref[r, pl.ds(base + a0 * SUBLANES, span * SUBLANES), lanes]
                big = big.reshape(span, SUBLANES, CONV_LANES)
                for j in taps:
                    a = (first_tap_row + j) // SUBLANES - a0
                    acc = acc + wb_ref[j, :, lanes] * big[a:a + groups]
            c_ref[pl.ds(base, CONV_ROWS), lanes] = acc.reshape(CONV_ROWS, CONV_LANES)
        return carry

    lax.fori_loop(0, tile // CONV_ROWS, conv_rows, 0)

    def norm_rows(i, carry):
        base = pl.multiple_of(i * NORM_ROWS, NORM_ROWS)
        c = c_ref[pl.ds(base, NORM_ROWS), :]
        mu = jnp.mean(c, axis=-1, keepdims=True)
        cen = c - mu
        var = jnp.mean(cen * cen, axis=-1, keepdims=True)
        y = cen * lax.rsqrt(var + NORM_EPS) * lng_ref[...] + lnb_ref[...]
        o_ref[pl.ds(base, NORM_ROWS), :] = _silu(y).astype(o_ref.dtype)
        return carry

    lax.fori_loop(0, tile // NORM_ROWS, norm_rows, 0)


def _conv_core(u, w_dw, b_dw, ln_g, ln_b, batch, seq):
    n, d = u.shape
    nt = seq // CONV_TILE
    row = lambda b, t: (b * nt + t, 0)
    return pl.pallas_call(
        _conv_kernel,
        grid=(batch, nt),
        in_specs=[pl.BlockSpec((CONV_TILE, d), row), _resident((CONV_WIDTH, d)),
                  _resident((1, d)), _resident((1, d)), _resident((1, d))],
        out_specs=pl.BlockSpec((CONV_TILE, d), row),
        out_shape=jax.ShapeDtypeStruct((n, d), BF16),
        scratch_shapes=[pltpu.VMEM((SUBLANES, CONV_TILE + CONV_HALO, d), F32),
                        pltpu.VMEM((CONV_WIDTH, SUBLANES, d), F32),
                        pltpu.VMEM((CONV_TILE, d), F32)],
        compiler_params=_params("arbitrary", "arbitrary"),
        name="conv_core",
    )(u, w_dw, b_dw, ln_g, ln_b)


def _hgrn_constants():
    c = HGRN_CHUNK
    t = jnp.arange(c)[:, None]
    s = jnp.arange(c)[None, :]
    tri = (s <= t).astype(BF16)
    masks = []
    for hs in HGRN_LEVELS[1:]:
        same = (t // (2 * hs)) == (s // (2 * hs))
        masks.append(same & ((t // hs) % 2 == 1) & ((s // hs) % 2 == 0))
    masks.append(t == s)
    return tri, jnp.stack(masks).astype(F32)


def _hgrn_kernel(x_ref, lbp_ref, ng_ref, tri_ref, mask_ref, o_ref, q_s, k_s, b_s, st_ref, *, layer):
    chunk = HGRN_CHUNK
    blocks = chunk // SUBLANES

    @pl.when(pl.program_id(1) == 0)
    def _():
        st_ref[...] = jnp.zeros(st_ref.shape, F32)

    p = lbp_ref[...]
    e = jnp.exp(p - jnp.max(p, axis=0, keepdims=True))
    sm = e / jnp.sum(e, axis=0, keepdims=True)
    lb = jnp.zeros((1, D_MODEL), F32)
    for l in range(1, layer + 1):
        lb = lb + sm[l:l + 1, :]

    qr = x_ref[:, 0:D_MODEL]
    q_s[...] = _silu(qr) * (HGRN_DK ** -0.5)
    f = lb + (1.0 - lb) * _sigmoid(x_ref[:, D_MODEL:2 * D_MODEL])
    k_s[...] = 1.0 - f
    logf = jnp.log(f)
    g1 = logf.astype(BF16)
    r1 = logf - g1.astype(F32)
    g2 = r1.astype(BF16)
    g3 = (r1 - g2.astype(F32)).astype(BF16)
    tri = tri_ref[...]
    b_s[...] = (_dot(tri, g1) + _dot(tri, g2) + _dot(tri, g3)) * LOG2E

    row_in_blk = lax.broadcasted_iota(jnp.int32, (blocks, SUBLANES, HGRN_DK), 1)
    upper_of_8 = row_in_blk >= 4
    sign4 = jnp.where(upper_of_8, 1.0, -1.0)
    sign2 = jnp.where(row_in_blk % 4 >= 2, 1.0, -1.0)

    def scores(hd):
        sl = slice(hd * HGRN_DK, (hd + 1) * HGRN_DK)
        qh, kh, bh = q_s[:, sl], k_s[:, sl], b_s[:, sl]
        kb = kh.astype(BF16)
        state = st_ref[hd]

        o = _dot_nt((qh * jnp.exp2(bh)).astype(BF16), state.astype(BF16))

        a = jnp.zeros((chunk, chunk), F32)
        for lvl, hs in enumerate(HGRN_LEVELS[:4]):
            q_parts, k_parts = [], []
            zeros = jnp.zeros((hs, HGRN_DK), F32)
            for s0 in range(0, chunk, 2 * hs):
                pivot = bh[s0 + hs - 1:s0 + hs, :]
                lo, up = slice(s0, s0 + hs), slice(s0 + hs, s0 + 2 * hs)
                q_parts += [zeros, qh[up] * jnp.exp2(bh[up] - pivot)]
                k_parts += [kh[lo] * jnp.exp2(pivot - bh[lo]), zeros]
            al = _dot_nt(jnp.concatenate(q_parts, axis=0).astype(BF16),
                         jnp.concatenate(k_parts, axis=0).astype(BF16))
            a = a + (al if lvl == 0 else al * mask_ref[lvl - 1])

        b3 = bh.reshape(blocks, SUBLANES, HGRN_DK)
        q3 = qh.reshape(blocks, SUBLANES, HGRN_DK)
        k3 = kh.reshape(blocks, SUBLANES, HGRN_DK)
        pivot4 = b3[:, 3:4, :]
        pivot2 = jnp.where(upper_of_8, b3[:, 5:6, :], b3[:, 1:2, :])
        for lvl, w3 in ((4, jnp.exp2((b3 - pivot4) * sign4)), (5, jnp.exp2((b3 - pivot2) * sign2))):
            al = _dot_nt((q3 * w3).reshape(chunk, HGRN_DK).astype(BF16),
                         (k3 * w3).reshape(chunk, HGRN_DK).astype(BF16))
            a = a + al * mask_ref[lvl - 1]
        a = a + _dot_nt((qh * (1.0 - kh)).astype(BF16), kb) * mask_ref[5]
        a = a + _dot_nt(qh.astype(BF16), kb) * mask_ref[6]
        return o, a, state

    def mix(hd, o, a, state):
        sl = slice(hd * HGRN_DK, (hd + 1) * HGRN_DK)
        kh, bh = k_s[:, sl], b_s[:, sl]
        vh = x_ref[:, 2 * D_MODEL + hd * HGRN_DK:2 * D_MODEL + (hd + 1) * HGRN_DK]
        b_last = bh[chunk - 1:chunk, :]
        o = o + _dot(a.astype(BF16), vh.astype(BF16))
        k_dec = (kh * jnp.exp2(b_last - bh)).astype(BF16)
        st_ref[hd] = state * jnp.exp2(b_last) + _dot(vh.T.astype(BF16), k_dec)
        return o

    def readout(hd, o):
        sl = slice(hd * HGRN_DK, (hd + 1) * HGRN_DK)
        gr = x_ref[:, 3 * D_MODEL + hd * HGRN_DK:3 * D_MODEL + (hd + 1) * HGRN_DK]
        on = o * lax.rsqrt(jnp.mean(o * o, axis=-1, keepdims=True) + NORM_EPS)
        o_ref[:, sl] = (on * ng_ref[:, sl] * _silu(gr)).astype(o_ref.dtype)

    stage1, stage2 = {}, {}
    for i in range(HGRN_HEADS + 2 * HEAD_SKEW):
        if i < HGRN_HEADS:
            stage1[i] = scores(i)
        if 0 <= i - HEAD_SKEW < HGRN_HEADS:
            stage2[i - HEAD_SKEW] = mix(i - HEAD_SKEW, *stage1.pop(i - HEAD_SKEW))
        if 0 <= i - 2 * HEAD_SKEW < HGRN_HEADS:
            readout(i - 2 * HEAD_SKEW, stage2.pop(i - 2 * HEAD_SKEW))


def _hgrn_core(qfig, lower_bounds, norm_g, layer, batch, seq):
    n = qfig.shape[0]
    nc = seq // HGRN_CHUNK
    tri, masks = _hgrn_constants()
    return pl.pallas_call(
        functools.partial(_hgrn_kernel, layer=layer),
        grid=(batch, nc),
        in_specs=[pl.BlockSpec((HGRN_CHUNK, 4 * D_MODEL), lambda b, c: (b * nc + c, 0)),
                  _resident(lower_bounds.shape), _resident((1, D_MODEL)),
                  _resident(tri.shape), _resident(masks.shape)],
        out_specs=pl.BlockSpec((HGRN_CHUNK, D_MODEL), lambda b, c: (b * nc + c, 0)),
        out_shape=jax.ShapeDtypeStruct((n, D_MODEL), BF16),
        scratch_shapes=[pltpu.VMEM((HGRN_CHUNK, D_MODEL), F32)] * 3
                       + [pltpu.VMEM((HGRN_HEADS, HGRN_DK, HGRN_DK), F32)],
        compiler_params=_params("arbitrary", "arbitrary"),
        name="hgrn_core",
    )(qfig, lower_bounds, norm_g, tri, masks)


def _resid_mlp_kernel(h_ref, y_ref, wo_ref, bo_ref, g_ref, w1_ref, w2_ref, gf_ref, o_ref, *,
                      has_bias, final_norm):
    h2 = h_ref[...] + _dot(y_ref[...], wo_ref[...])
    if has_bias:
        h2 = h2 + bo_ref[...]
    hn = _rms_norm(h2, g_ref[...]).astype(BF16)
    acc = h2
    for c in range(0, D_FF, FF_CHUNK):
        up = jnp.maximum(_dot(hn, w1_ref[:, c:c + FF_CHUNK]), 0.0)
        acc = acc + _dot((up * up).astype(BF16), w2_ref[c:c + FF_CHUNK, :])
    if final_norm:
        acc = _rms_norm(acc, gf_ref[...])
    o_ref[...] = acc


def _resid_mlp(h, y, wo_stack, wo_layer, bo, g, w1_stack, w2_stack, layer, gf, has_bias, final_norm):
    n, d = h.shape
    row = lambda i: (i, 0)
    return pl.pallas_call(
        functools.partial(_resid_mlp_kernel, has_bias=has_bias, final_norm=final_norm),
        grid=(n // ROW_TILE,),
        in_specs=[pl.BlockSpec((ROW_TILE, d), row), pl.BlockSpec((ROW_TILE, d), row),
                  _resident(wo_stack.shape, wo_layer), _resident((1, d)), _resident((1, d)),
                  _resident(w1_stack.shape, layer), _resident(w2_stack.shape, layer), _resident((1, d))],
        out_specs=pl.BlockSpec((ROW_TILE, d), row),
        out_shape=jax.ShapeDtypeStruct((n, d), F32),
        compiler_params=_params("parallel"),
        name="resid_mlp",
    )(h, y, wo_stack, bo, g, w1_stack, w2_stack, gf)


def kernel(x, attn_w_qkv, attn_sinks, attn_w_o, conv_w_pw1, conv_b_pw1, conv_w_dw, conv_b_dw, conv_ln_g,
           conv_ln_b, conv_w_pw2, conv_b_pw2, hgrn_w_qfig, hgrn_lower_bounds, hgrn_norm_g, hgrn_w_o,
           norm_mixer, norm_mlp, mlp_w1, mlp_w2, final_norm):
    batch, seq, d = x.shape
    h = x.reshape(batch * seq, d)
    row = lambda v: v.reshape(1, -1)
    zero_bias = jnp.zeros((1, d), F32)
    w_qkv, w_ao = attn_w_qkv.astype(BF16), attn_w_o.astype(BF16)
    w_pw1, w_pw2 = conv_w_pw1.astype(BF16), conv_w_pw2.astype(BF16)
    w_qfig, w_ho = hgrn_w_qfig.astype(BF16), hgrn_w_o.astype(BF16)
    w1, w2 = mlp_w1.astype(BF16), mlp_w2.astype(BF16)
    for i in range(DEPTH):
        kind, j = i % N_MIXERS, i // N_MIXERS
        g_mix = row(norm_mixer[i])
        bias, has_bias = zero_bias, False
        if kind == 0:
            qkv = _norm_proj(h, g_mix, w_qkv, j, BF16)
            y = _attention_core(qkv, attn_sinks[j], batch, seq)
            wo = w_ao
        elif kind == 1:
            u = _norm_glu(h, g_mix, w_pw1, j, row(conv_b_pw1[j]))
            y = _conv_core(u, conv_w_dw[j], row(conv_b_dw[j]), row(conv_ln_g[j]), row(conv_ln_b[j]), batch, seq)
            wo, bias, has_bias = w_pw2, row(conv_b_pw2[j]), True
        else:
            qfig = _norm_proj(h, g_mix, w_qfig, j, F32)
            y = _hgrn_core(qfig, hgrn_lower_bounds, row(hgrn_norm_g[j]), i, batch, seq)
            wo = w_ho
        h = _resid_mlp(h, y, wo, j, bias, row(norm_mlp[i]), w1, w2, i, row(final_norm),
                       has_bias, i == DEPTH - 1)
    return h.reshape(batch, seq, d)
```

```python
import functools
import math

import jax
import jax.numpy as jnp
from jax import lax
from jax.experimental import pallas as pl
from jax.experimental.pallas import tpu as pltpu

F32 = jnp.float32
BF16 = jnp.bfloat16

D_MODEL = 1024
DEPTH = 4
N_MIXERS = 3
ATTN_HEADS = 16
ATTN_KV_HEADS = 2
HEAD_DIM = 64
WINDOW = 128
QKV_DIM = (ATTN_HEADS + 2 * ATTN_KV_HEADS) * HEAD_DIM
CONV_WIDTH = 31
HGRN_HEADS = 8
HGRN_DK = 128
D_FF = 4 * D_MODEL
NORM_EPS = 1e-6
MASK_VALUE = -1e30
LOG2E = math.log2(math.e)

V7X_VMEM_BYTES = 64 * 1024 * 1024
VMEM_LIMIT_BYTES = V7X_VMEM_BYTES - 8 * 1024 * 1024
SUBLANES = 8
LANES = 128

ROW_TILE = 1024
FF_CHUNK = 1024
WEIGHT_CHUNK_BYTES = 1024 * 1024
ATTN_TILE = 512
CONV_TILE = 512
CONV_HALO = 32
CONV_ROWS = 32
CONV_LANES = 128
NORM_ROWS = 256
HGRN_CHUNK = 128
HGRN_STEP_CHUNKS = 4
HGRN_LEVELS = (64, 32, 16, 8, 4, 2, 1)
HEAD_SKEW = 3


def _params(*semantics):
    return pltpu.CompilerParams(dimension_semantics=semantics, vmem_limit_bytes=VMEM_LIMIT_BYTES)


def _resident(shape, lead=None):
    if lead is None:
        return pl.BlockSpec(shape, lambda *_: (0,) * len(shape), pipeline_mode=pl.Buffered(1))
    return pl.BlockSpec((None,) + tuple(shape[1:]), lambda *_: (lead,) + (0,) * (len(shape) - 1),
                        pipeline_mode=pl.Buffered(1))


def _rms_norm(x, g):
    ms = jnp.mean(x * x, axis=-1, keepdims=True)
    return x * lax.rsqrt(ms + NORM_EPS) * g


def _sigmoid(x):
    return 0.5 * jnp.tanh(0.5 * x) + 0.5


def _silu(x):
    h = 0.5 * x
    return h + h * jnp.tanh(h)


def _zero_like(x):
    bits = lax.bitcast_convert_type(x, jnp.uint32)
    sixteen = jnp.uint32(16)
    return lax.bitcast_convert_type((bits >> sixteen) >> sixteen, F32)


def _dot(a, b):
    return jnp.dot(a, b, preferred_element_type=F32)


def _dot_nt(a, b):
    return lax.dot_general(a, b, (((1,), (1,)), ((), ())), preferred_element_type=F32)


def _norm_proj_kernel(x_ref, g_ref, w_ref, o_ref):
    hn = _rms_norm(x_ref[...], g_ref[...]).astype(BF16)
    n_out = o_ref.shape[1]
    chunk = min(n_out, FF_CHUNK)
    for c in range(0, n_out, chunk):
        width = min(chunk, n_out - c)
        o_ref[:, c:c + width] = _dot(hn, w_ref[:, c:c + width]).astype(o_ref.dtype)


def _norm_glu_kernel(x_ref, g_ref, w_ref, b_ref, o_ref):
    hn = _rms_norm(x_ref[...], g_ref[...]).astype(BF16)
    a = _dot(hn, w_ref[:, :D_MODEL]) + b_ref[:, :D_MODEL]
    gate = _dot(hn, w_ref[:, D_MODEL:]) + b_ref[:, D_MODEL:]
    o_ref[...] = a * _sigmoid(gate)


def _norm_proj(h, g, w_stack, layer, out_dtype):
    n, d = h.shape
    n_out = w_stack.shape[2]
    return pl.pallas_call(
        _norm_proj_kernel,
        grid=(n // ROW_TILE,),
        in_specs=[pl.BlockSpec((ROW_TILE, d), lambda i: (i, 0)), _resident((1, d)),
                  _resident(w_stack.shape, layer)],
        out_specs=pl.BlockSpec((ROW_TILE, n_out), lambda i: (i, 0)),
        out_shape=jax.ShapeDtypeStruct((n, n_out), out_dtype),
        compiler_params=_params("parallel"),
        name="norm_proj",
    )(h, g, w_stack)


def _norm_glu(h, g, w_stack, layer, b):
    n, d = h.shape
    return pl.pallas_call(
        _norm_glu_kernel,
        grid=(n // ROW_TILE,),
        in_specs=[pl.BlockSpec((ROW_TILE, d), lambda i: (i, 0)), _resident((1, d)),
                  _resident(w_stack.shape, layer), _resident((1, 2 * d))],
        out_specs=pl.BlockSpec((ROW_TILE, d), lambda i: (i, 0)),
        out_shape=jax.ShapeDtypeStruct((n, d), F32),
        compiler_params=_params("parallel"),
        name="norm_glu",
    )(h, g, w_stack, b)


def _attn_bias():
    qi = jnp.arange(WINDOW)[:, None]
    col = jnp.arange(4 * WINDOW)[None, :]
    si = (col // (2 * WINDOW)) * WINDOW + col % WINDOW
    second = (col // WINDOW) % 2
    dist = WINDOW + qi - si
    valid = (dist >= 0) & (dist < WINDOW)
    head = 2 * jnp.arange(ATTN_HEADS // 2)[:, None, None] + second[None]
    slopes = jnp.exp2(-8.0 * (head + 1).astype(F32) / ATTN_HEADS)
    return jnp.where(valid[None], -(slopes * dist[None].astype(F32)), MASK_VALUE)


def _attn_kernel(sink_ref, q_ref, kvc_ref, kvp_ref, bias_ref, o_ref):
    first_tile = pl.program_id(1) == 0
    pair = 2 * HEAD_DIM
    nblk = ATTN_TILE // WINDOW
    group = ATTN_HEADS // ATTN_KV_HEADS
    scale = HEAD_DIM ** -0.5
    low = lax.broadcasted_iota(jnp.int32, (WINDOW, pair), 1) < HEAD_DIM
    ones_bd = jnp.concatenate([jnp.where(low, 1.0, 0.0), jnp.where(low, 0.0, 1.0)], axis=0).astype(BF16)
    prev_cols = lax.broadcasted_iota(jnp.int32, (WINDOW, 4 * WINDOW), 1) < 2 * WINDOW

    def expand(x, xr, kvh):
        first, second = (x, xr) if kvh == 0 else (xr, x)
        return jnp.concatenate([jnp.where(low, first, 0.0), jnp.where(low, 0.0, second)], axis=0).astype(BF16)

    kblk, vblk = [], []
    for r in range(nblk + 1):
        kv = (kvp_ref[...] if r == 0 else kvc_ref[(r - 1) * WINDOW:r * WINDOW, :]).astype(F32)
        k2, v2 = kv[:, :pair] * scale, kv[:, pair:]
        k2r, v2r = pltpu.roll(k2, HEAD_DIM, 1), pltpu.roll(v2, HEAD_DIM, 1)
        kblk.append([expand(k2, k2r, kvh) for kvh in range(ATTN_KV_HEADS)])
        vblk.append([jnp.concatenate([expand(v2, v2r, kvh), ones_bd], axis=1) for kvh in range(ATTN_KV_HEADS)])

    kb = [[jnp.concatenate([kblk[r][kvh], kblk[r + 1][kvh]], axis=0) for kvh in range(ATTN_KV_HEADS)]
          for r in range(nblk)]
    vb = [[jnp.concatenate([vblk[r][kvh], vblk[r + 1][kvh]], axis=0) for kvh in range(ATTN_KV_HEADS)]
          for r in range(nblk)]

    def scores(r, j):
        rows, cols = slice(r * WINDOW, (r + 1) * WINDOW), slice(j * pair, (j + 1) * pair)
        s = _dot_nt(q_ref[rows, cols], kb[r][(2 * j) // group]) + bias_ref[j]
        if r == 0:
            s = jnp.where(first_tile & prev_cols, MASK_VALUE, s)
        return s

    def weigh(r, j, s):
        rows, cols = slice(r * WINDOW, (r + 1) * WINDOW), slice(j * pair, (j + 1) * pair)
        sa0, sb0, sa1, sb1 = (s[:, i * WINDOW:(i + 1) * WINDOW] for i in range(4))
        sink_a, sink_b = sink_ref[2 * j], sink_ref[2 * j + 1]
        ma = jnp.maximum(jnp.max(jnp.maximum(sa0, sa1), axis=-1, keepdims=True), sink_a)
        mb = jnp.maximum(jnp.max(jnp.maximum(sb0, sb1), axis=-1, keepdims=True), sink_b)
        e = jnp.concatenate([jnp.exp(sa0 - ma), jnp.exp(sb0 - mb), jnp.exp(sa1 - ma), jnp.exp(sb1 - mb)],
                            axis=1).astype(BF16)
        ov = _dot(e, vb[r][(2 * j) // group])
        sink_term = jnp.where(low, jnp.exp(sink_a - ma), jnp.exp(sink_b - mb))
        o_ref[rows, cols] = (ov[:, :pair] / (ov[:, pair:] + sink_term)).astype(o_ref.dtype)

    for r in range(nblk):
        for j in range(ATTN_HEADS // 2):
            weigh(r, j, scores(r, j))


def _attention_core(qkv, sinks, batch, seq):
    n = batch * seq
    nt = seq // ATTN_TILE
    blocks_per_tile = ATTN_TILE // WINDOW
    q_cols = ATTN_HEADS * HEAD_DIM
    kv_cols = 2 * ATTN_KV_HEADS * HEAD_DIM
    kv_col_block = q_cols // kv_cols
    bias = _attn_bias()

    def prev_block(b, i):
        return ((b * nt + i) * blocks_per_tile - jnp.minimum(i, 1), kv_col_block)

    return pl.pallas_call(
        _attn_kernel,
        grid=(batch, nt),
        in_specs=[
            pl.BlockSpec(memory_space=pltpu.SMEM),
            pl.BlockSpec((ATTN_TILE, q_cols), lambda b, i: (b * nt + i, 0)),
            pl.BlockSpec((ATTN_TILE, kv_cols), lambda b, i: (b * nt + i, kv_col_block)),
            pl.BlockSpec((WINDOW, kv_cols), prev_block),
            _resident(bias.shape),
        ],
        out_specs=pl.BlockSpec((ATTN_TILE, q_cols), lambda b, i: (b * nt + i, 0)),
        out_shape=jax.ShapeDtypeStruct((n, q_cols), BF16),
        compiler_params=_params("parallel", "parallel"),
        name="attn_core",
    )(sinks, qkv, qkv, qkv, bias)


def _conv_kernel(u_ref, w_ref, bdw_ref, lng_ref, lnb_ref, o_ref, buf_ref, wb_ref, c_ref):
    t = pl.program_id(1)
    tile = u_ref.shape[0]

    @pl.when((pl.program_id(0) == 0) & (t == 0))
    def _():
        for j in range(CONV_WIDTH):
            wb_ref[j] = jnp.broadcast_to(w_ref[j:j + 1, :], (SUBLANES, D_MODEL))

    @pl.when(t == 0)
    def _():
        buf_ref[0, 0:CONV_HALO, :] = jnp.zeros((CONV_HALO, D_MODEL), F32)

    @pl.when(t > 0)
    def _():
        buf_ref[0, 0:CONV_HALO, :] = buf_ref[0, tile:tile + CONV_HALO, :]

    buf_ref[0, CONV_HALO:, :] = u_ref[...]
    shifted_rows = tile + CONV_HALO - SUBLANES
    for r in range(1, SUBLANES):
        buf_ref[r, 0:shifted_rows, :] = buf_ref[0, r:r + shifted_rows, :]

    first_tap_row = CONV_HALO - (CONV_WIDTH - 1)
    groups = CONV_ROWS // SUBLANES

    def conv_rows(i, carry):
        base = pl.multiple_of(i * CONV_ROWS, CONV_ROWS)
        acc = None
        for l in range(0, D_MODEL, CONV_LANES):
            lanes = slice(l, l + CONV_LANES)
            start = bdw_ref[:, lanes]
            if acc is not None:
                start = start + _zero_like(acc[0])
            acc = jnp.broadcast_to(start, (groups, SUBLANES, CONV_LANES))
            for r in range(SUBLANES):
                taps = [j for j in range(CONV_WIDTH) if (first_tap_row + j) % SUBLANES == r]
                a0 = (first_tap_row + taps[0]) // SUBLANES
                span = (first_tap_row + taps[-1]) // SUBLANES - a0 + groups
                big = buf_ref[r, pl.ds(base + a0 * SUBLANES, span * SUBLANES), lanes]
                big = big.reshape(span, SUBLANES, CONV_LANES)
                for j in taps:
                    a = (first_tap_row + j) // SUBLANES - a0
                    acc = acc + wb_ref[j, :, lanes] * big[a:a + groups]
            c_ref[pl.ds(base, CONV_ROWS), lanes] = acc.reshape(CONV_ROWS, CONV_LANES)
        return carry

    lax.fori_loop(0, tile // CONV_ROWS, conv_rows, 0)

    def norm_rows(i, carry):
        base = pl.multiple_of(i * NORM_ROWS, NORM_ROWS)
        c = c_ref[pl.ds(base, NORM_ROWS), :]
        mu = jnp.mean(c, axis=-1, keepdims=True)
        cen = c - mu
        var = jnp.mean(cen * cen, axis=-1, keepdims=True)
        y = cen * lax.rsqrt(var + NORM_EPS) * lng_ref[...] + lnb_ref[...]
        o_ref[pl.ds(base, NORM_ROWS), :] = _silu(y).astype(o_ref.dtype)
        return carry

    lax.fori_loop(0, tile // NORM_ROWS, norm_rows, 0)


def _conv_core(u, w_dw, b_dw, ln_g, ln_b, batch, seq):
    n, d = u.shape
    nt = seq // CONV_TILE
    row = lambda b, t: (b * nt + t, 0)
    return pl.pallas_call(
        _conv_kernel,
        grid=(batch, nt),
        in_specs=[pl.BlockSpec((CONV_TILE, d), row), _resident((CONV_WIDTH, d)),
                  _resident((1, d)), _resident((1, d)), _resident((1, d))],
        out_specs=pl.BlockSpec((CONV_TILE, d), row),
        out_shape=jax.ShapeDtypeStruct((n, d), BF16),
        scratch_shapes=[pltpu.VMEM((SUBLANES, CONV_TILE + CONV_HALO, d), F32),
                        pltpu.VMEM((CONV_WIDTH, SUBLANES, d), F32),
                        pltpu.VMEM((CONV_TILE, d), F32)],
        compiler_params=_params("arbitrary", "arbitrary"),
        name="conv_core",
    )(u, w_dw, b_dw, ln_g, ln_b)


def _hgrn_constants():
    c = HGRN_CHUNK
    t = jnp.arange(c)[:, None]
    s = jnp.arange(c)[None, :]
    tri = (s <= t).astype(BF16)
    masks = []
    for hs in HGRN_LEVELS[1:]:
        same = (t // (2 * hs)) == (s // (2 * hs))
        masks.append(same & ((t // hs) % 2 == 1) & ((s // hs) % 2 == 0))
    masks.append(t == s)
    return tri, jnp.stack(masks).astype(F32)


def _hgrn_kernel(x_ref, lbp_ref, ng_ref, tri_ref, mask_ref, o_ref, q_s, k_s, b_s, st_ref, *, layer):
    chunk = HGRN_CHUNK
    blocks = chunk // SUBLANES

    @pl.when(pl.program_id(1) == 0)
    def _():
        st_ref[...] = jnp.zeros(st_ref.shape, F32)

    p = lbp_ref[...]
    e = jnp.exp(p - jnp.max(p, axis=0, keepdims=True))
    sm = e / jnp.sum(e, axis=0, keepdims=True)
    lb = jnp.zeros((1, D_MODEL), F32)
    for l in range(1, layer + 1):
        lb = lb + sm[l:l + 1, :]

    def prepare(c):
        rows = slice(c * chunk, (c + 1) * chunk)
        q_s[c] = _silu(x_ref[rows, 0:D_MODEL]) * (HGRN_DK ** -0.5)
        f = lb + (1.0 - lb) * _sigmoid(x_ref[rows, D_MODEL:2 * D_MODEL])
        k_s[c] = 1.0 - f
        logf = jnp.log(f)
        g1 = logf.astype(BF16)
        r1 = logf - g1.astype(F32)
        g2 = r1.astype(BF16)
        g3 = (r1 - g2.astype(F32)).astype(BF16)
        tri = tri_ref[...]
        b_s[c] = (_dot(tri, g1) + _dot(tri, g2) + _dot(tri, g3)) * LOG2E

    row_in_blk = lax.broadcasted_iota(jnp.int32, (blocks, SUBLANES, HGRN_DK), 1)
    upper_of_8 = row_in_blk >= 4
    sign4 = jnp.where(upper_of_8, 1.0, -1.0)
    sign2 = jnp.where(row_in_blk % 4 >= 2, 1.0, -1.0)

    def scores(c, hd):
        sl = slice(hd * HGRN_DK, (hd + 1) * HGRN_DK)
        qh, kh, bh = q_s[c, :, sl], k_s[c, :, sl], b_s[c, :, sl]
        kb = kh.astype(BF16)
        state = st_ref[hd]

        o = _dot_nt((qh * jnp.exp2(bh)).astype(BF16), state.astype(BF16))

        a = jnp.zeros((chunk, chunk), F32)
        for lvl, hs in enumerate(HGRN_LEVELS[:4]):
            q_parts, k_parts = [], []
            zeros = jnp.zeros((hs, HGRN_DK), F32)
            for s0 in range(0, chunk, 2 * hs):
                pivot = bh[s0 + hs - 1:s0 + hs, :]
                lo, up = slice(s0, s0 + hs), slice(s0 + hs, s0 + 2 * hs)
                q_parts += [zeros, qh[up] * jnp.exp2(bh[up] - pivot)]
                k_parts += [kh[lo] * jnp.exp2(pivot - bh[lo]), zeros]
            al = _dot_nt(jnp.concatenate(q_parts, axis=0).astype(BF16),
                         jnp.concatenate(k_parts, axis=0).astype(BF16))
            a = a + (al if lvl == 0 else al * mask_ref[lvl - 1])

        b3 = bh.reshape(blocks, SUBLANES, HGRN_DK)
        q3 = qh.reshape(blocks, SUBLANES, HGRN_DK)
        k3 = kh.reshape(blocks, SUBLANES, HGRN_DK)
        pivot4 = b3[:, 3:4, :]
        pivot2 = jnp.where(upper_of_8, b3[:, 5:6, :], b3[:, 1:2, :])
        for lvl, w3 in ((4, jnp.exp2((b3 - pivot4) * sign4)), (5, jnp.exp2((b3 - pivot2) * sign2))):
            al = _dot_nt((q3 * w3).reshape(chunk, HGRN_DK).astype(BF16),
                         (k3 * w3).reshape(chunk, HGRN_DK).astype(BF16))
            a = a + al * mask_ref[lvl - 1]
        a = a + _dot_nt((qh * (1.0 - kh)).astype(BF16), kb) * mask_ref[5]
        a = a + _dot_nt(qh.astype(BF16), kb) * mask_ref[6]
        return o, a, state

    def mix(c, hd, o, a, state):
        sl = slice(hd * HGRN_DK, (hd + 1) * HGRN_DK)
        rows = slice(c * chunk, (c + 1) * chunk)
        kh, bh = k_s[c, :, sl], b_s[c, :, sl]
        vh = x_ref[rows, 2 * D_MODEL + hd * HGRN_DK:2 * D_MODEL + (hd + 1) * HGRN_DK]
        b_last = bh[chunk - 1:chunk, :]
        o = o + _dot(a.astype(BF16), vh.astype(BF16))
        k_dec = (kh * jnp.exp2(b_last - bh)).astype(BF16)
        st_ref[hd] = state * jnp.exp2(b_last) + _dot(vh.T.astype(BF16), k_dec)
        return o

    def readout(c, hd, o):
        sl = slice(hd * HGRN_DK, (hd + 1) * HGRN_DK)
        rows = slice(c * chunk, (c + 1) * chunk)
        gr = x_ref[rows, 3 * D_MODEL + hd * HGRN_DK:3 * D_MODEL + (hd + 1) * HGRN_DK]
        on = o * lax.rsqrt(jnp.mean(o * o, axis=-1, keepdims=True) + NORM_EPS)
        o_ref[rows, sl] = (on * ng_ref[:, sl] * _silu(gr)).astype(o_ref.dtype)

    items = [(c, hd) for c in range(HGRN_STEP_CHUNKS) for hd in range(HGRN_HEADS)]
    prepare(0)
    stage1, stage2 = {}, {}
    for i in range(len(items) + 2 * HEAD_SKEW):
        if i < len(items):
            stage1[i] = scores(*items[i])
        if 0 <= i - HEAD_SKEW < len(items):
            stage2[i - HEAD_SKEW] = mix(*items[i - HEAD_SKEW], *stage1.pop(i - HEAD_SKEW))
        if 0 <= i - 2 * HEAD_SKEW < len(items):
            readout(*items[i - 2 * HEAD_SKEW], stage2.pop(i - 2 * HEAD_SKEW))
        if 1 <= i < HGRN_STEP_CHUNKS:
            prepare(i)


def _hgrn_core(qfig, lower_bounds, norm_g, layer, batch, seq):
    n = qfig.shape[0]
    rows = HGRN_STEP_CHUNKS * HGRN_CHUNK
    nc = seq // rows
    tri, masks = _hgrn_constants()
    return pl.pallas_call(
        functools.partial(_hgrn_kernel, layer=layer),
        grid=(batch, nc),
        in_specs=[pl.BlockSpec((rows, 4 * D_MODEL), lambda b, c: (b * nc + c, 0)),
                  _resident(lower_bounds.shape), _resident((1, D_MODEL)),
                  _resident(tri.shape), _resident(masks.shape)],
        out_specs=pl.BlockSpec((rows, D_MODEL), lambda b, c: (b * nc + c, 0)),
        out_shape=jax.ShapeDtypeStruct((n, D_MODEL), BF16),
        scratch_shapes=[pltpu.VMEM((HGRN_STEP_CHUNKS, HGRN_CHUNK, D_MODEL), F32)] * 3
                       + [pltpu.VMEM((HGRN_HEADS, HGRN_DK, HGRN_DK), F32)],
        compiler_params=_params("arbitrary", "arbitrary"),
        name="hgrn_core",
    )(qfig, lower_bounds, norm_g, tri, masks)


def _load_weight(w_hbm, layer, dst_ref, stage_ref, sem):
    rows = stage_ref.shape[1]
    n_chunks = dst_ref.shape[0] // rows

    def copy(k):
        return pltpu.make_async_copy(w_hbm.at[layer, pl.ds(k * rows, rows), :], stage_ref.at[k % 2], sem.at[k % 2])

    copy(0).start()
    for k in range(n_chunks):
        if k + 1 < n_chunks:
            copy(k + 1).start()
        copy(k).wait()
        dst_ref[k * rows:(k + 1) * rows, :] = stage_ref[k % 2].astype(BF16)


def _resid_mlp_kernel(h_ref, y_ref, wo_ref, bo_ref, g_ref, w1_hbm, w2_hbm, gf_ref, o_ref,
                      w1_ref, w2_ref, stage1_ref, stage2_ref, sem, *, layer, has_bias, final_norm):
    @pl.when(pl.program_id(0) == 0)
    def _():
        _load_weight(w1_hbm, layer, w1_ref, stage1_ref, sem.at[0])
        _load_weight(w2_hbm, layer, w2_ref, stage2_ref, sem.at[1])

    h2 = h_ref[...] + _dot(y_ref[...], wo_ref[...])
    if has_bias:
        h2 = h2 + bo_ref[...]
    hn = _rms_norm(h2, g_ref[...]).astype(BF16)
    acc = h2
    for c in range(0, D_FF, FF_CHUNK):
        up = jnp.maximum(_dot(hn, w1_ref[:, c:c + FF_CHUNK]), 0.0)
        acc = acc + _dot((up * up).astype(BF16), w2_ref[c:c + FF_CHUNK, :])
    if final_norm:
        acc = _rms_norm(acc, gf_ref[...])
    o_ref[...] = acc


def _resid_mlp(h, y, wo_stack, wo_layer, bo, g, w1_stack, w2_stack, layer, gf, has_bias, final_norm):
    n, d = h.shape
    row = lambda i: (i, 0)
    return pl.pallas_call(
        functools.partial(_resid_mlp_kernel, layer=layer, has_bias=has_bias, final_norm=final_norm),
        grid=(n // ROW_TILE,),
        in_specs=[pl.BlockSpec((ROW_TILE, d), row), pl.BlockSpec((ROW_TILE, d), row),
                  _resident(wo_stack.shape, wo_layer), _resident((1, d)), _resident((1, d)),
                  pl.BlockSpec(memory_space=pl.ANY), pl.BlockSpec(memory_space=pl.ANY), _resident((1, d))],
        out_specs=pl.BlockSpec((ROW_TILE, d), row),
        out_shape=jax.ShapeDtypeStruct((n, d), F32),
        scratch_shapes=[pltpu.VMEM((d, D_FF), BF16), pltpu.VMEM((D_FF, d), BF16),
                        pltpu.VMEM((2, WEIGHT_CHUNK_BYTES // (4 * D_FF), D_FF), F32),
                        pltpu.VMEM((2, WEIGHT_CHUNK_BYTES // (4 * d), d), F32),
                        pltpu.SemaphoreType.DMA((2, 2))],
        compiler_params=_params("arbitrary"),
        name="resid_mlp",
    )(h, y, wo_stack, bo, g, w1_stack, w2_stack, gf)


def kernel(x, attn_w_qkv, attn_sinks, attn_w_o, conv_w_pw1, conv_b_pw1, conv_w_dw, conv_b_dw, conv_ln_g,
           conv_ln_b, conv_w_pw2, conv_b_pw2, hgrn_w_qfig, hgrn_lower_bounds, hgrn_norm_g, hgrn_w_o,
           norm_mixer, norm_mlp, mlp_w1, mlp_w2, final_norm):
    batch, seq, d = x.shape
    h = x.reshape(batch * seq, d)
    row = lambda v: v.reshape(1, -1)
    zero_bias = jnp.zeros((1, d), F32)
    w_qkv, w_ao = attn_w_qkv.astype(BF16), attn_w_o.astype(BF16)
    w_pw1, w_pw2 = conv_w_pw1.astype(BF16), conv_w_pw2.astype(BF16)
    w_qfig, w_ho = hgrn_w_qfig.astype(BF16), hgrn_w_o.astype(BF16)
    for i in range(DEPTH):
        kind, j = i % N_MIXERS, i // N_MIXERS
        g_mix = row(norm_mixer[i])
        bias, has_bias = zero_bias, False
        if kind == 0:
            qkv = _norm_proj(h, g_mix, w_qkv, j, BF16)
            y = _attention_core(qkv, attn_sinks[j], batch, seq)
            wo = w_ao
        elif kind == 1:
            u = _norm_glu(h, g_mix, w_pw1, j, row(conv_b_pw1[j]))
            y = _conv_core(u, conv_w_dw[j], row(conv_b_dw[j]), row(conv_ln_g[j]), row(conv_ln_b[j]), batch, seq)
            wo, bias, has_bias = w_pw2, row(conv_b_pw2[j]), True
        else:
            qfig = _norm_proj(h, g_mix, w_qfig, j, F32)
            y = _hgrn_core(qfig, hgrn_lower_bounds, row(hgrn_norm_g[j]), i, batch, seq)
            wo = w_ho
        h = _resid_mlp(h, y, wo, j, bias, row(norm_mlp[i]), mlp_w1, mlp_w2, i, row(final_norm),
                       has_bias, i == DEPTH - 1)
    return h.reshape(batch, seq, d)
```

```python
import functools
import math

import jax
import jax.numpy as jnp
from jax import lax
from jax.experimental import pallas as pl
from jax.experimental.pallas import tpu as pltpu

F32 = jnp.float32
BF16 = jnp.bfloat16

D_MODEL = 1024
DEPTH = 4
N_MIXERS = 3
ATTN_HEADS = 16
ATTN_KV_HEADS = 2
HEAD_DIM = 64
WINDOW = 128
QKV_DIM = (ATTN_HEADS + 2 * ATTN_KV_HEADS) * HEAD_DIM
CONV_WIDTH = 31
HGRN_HEADS = 8
HGRN_DK = 128
D_FF = 4 * D_MODEL
NORM_EPS = 1e-6
MASK_VALUE = -1e30
LOG2E = math.log2(math.e)

V7X_VMEM_BYTES = 64 * 1024 * 1024
VMEM_LIMIT_BYTES = V7X_VMEM_BYTES - 8 * 1024 * 1024
SUBLANES = 8
LANES = 128

ROW_TILE = 1024
FF_CHUNK = 1024
ATTN_TILE = 1024
CONV_TILE = 512
CONV_HALO = 32
CONV_ROWS = 32
CONV_LANES = 128
NORM_ROWS = 256
HGRN_CHUNK = 128
HGRN_STEP_CHUNKS = 4
HGRN_LEVELS = (64, 32, 16, 8, 4, 2, 1)
HEAD_SKEW = 4


def _params(*semantics):
    return pltpu.CompilerParams(dimension_semantics=semantics, vmem_limit_bytes=VMEM_LIMIT_BYTES)


def _resident(shape, lead=None):
    if lead is None:
        return pl.BlockSpec(shape, lambda *_: (0,) * len(shape), pipeline_mode=pl.Buffered(1))
    return pl.BlockSpec((None,) + tuple(shape[1:]), lambda *_: (lead,) + (0,) * (len(shape) - 1),
                        pipeline_mode=pl.Buffered(1))


def _rms_norm(x, g):
    ms = jnp.mean(x * x, axis=-1, keepdims=True)
    return x * lax.rsqrt(ms + NORM_EPS) * g


def _sigmoid(x):
    return 0.5 * jnp.tanh(0.5 * x) + 0.5


def _silu(x):
    h = 0.5 * x
    return h + h * jnp.tanh(h)


def _zero_like(x):
    bits = lax.bitcast_convert_type(x, jnp.uint32)
    sixteen = jnp.uint32(16)
    return lax.bitcast_convert_type((bits >> sixteen) >> sixteen, F32)


def _dot(a, b):
    return jnp.dot(a, b, preferred_element_type=F32)


def _dot_nt(a, b):
    return lax.dot_general(a, b, (((1,), (1,)), ((), ())), preferred_element_type=F32)


def _norm_proj_kernel(x_ref, g_ref, w_ref, o_ref):
    hn = _rms_norm(x_ref[...], g_ref[...]).astype(BF16)
    n_out = o_ref.shape[1]
    chunk = min(n_out, FF_CHUNK)
    for c in range(0, n_out, chunk):
        width = min(chunk, n_out - c)
        o_ref[:, c:c + width] = _dot(hn, w_ref[:, c:c + width]).astype(o_ref.dtype)


def _norm_glu_kernel(x_ref, g_ref, w_ref, b_ref, o_ref):
    hn = _rms_norm(x_ref[...], g_ref[...]).astype(BF16)
    a = _dot(hn, w_ref[:, :D_MODEL]) + b_ref[:, :D_MODEL]
    gate = _dot(hn, w_ref[:, D_MODEL:]) + b_ref[:, D_MODEL:]
    o_ref[...] = a * _sigmoid(gate)


def _norm_proj(h, g, w_stack, layer, out_dtype):
    n, d = h.shape
    n_out = w_stack.shape[2]
    return pl.pallas_call(
        _norm_proj_kernel,
        grid=(n // ROW_TILE,),
        in_specs=[pl.BlockSpec((ROW_TILE, d), lambda i: (i, 0)), _resident((1, d)),
                  _resident(w_stack.shape, layer)],
        out_specs=pl.BlockSpec((ROW_TILE, n_out), lambda i: (i, 0)),
        out_shape=jax.ShapeDtypeStruct((n, n_out), out_dtype),
        compiler_params=_params("parallel"),
        name="norm_proj",
    )(h, g, w_stack)


def _norm_glu(h, g, w_stack, layer, b):
    n, d = h.shape
    return pl.pallas_call(
        _norm_glu_kernel,
        grid=(n // ROW_TILE,),
        in_specs=[pl.BlockSpec((ROW_TILE, d), lambda i: (i, 0)), _resident((1, d)),
                  _resident(w_stack.shape, layer), _resident((1, 2 * d))],
        out_specs=pl.BlockSpec((ROW_TILE, d), lambda i: (i, 0)),
        out_shape=jax.ShapeDtypeStruct((n, d), F32),
        compiler_params=_params("parallel"),
        name="norm_glu",
    )(h, g, w_stack, b)


def _attn_bias():
    qi = jnp.arange(WINDOW)[:, None]
    col = jnp.arange(4 * WINDOW)[None, :]
    si = (col // (2 * WINDOW)) * WINDOW + col % WINDOW
    second = (col // WINDOW) % 2
    dist = WINDOW + qi - si
    valid = (dist >= 0) & (dist < WINDOW)
    head = 2 * jnp.arange(ATTN_HEADS // 2)[:, None, None] + second[None]
    slopes = jnp.exp2(-8.0 * (head + 1).astype(F32) / ATTN_HEADS)
    return jnp.where(valid[None], -(slopes * dist[None].astype(F32)), MASK_VALUE)


def _attn_kernel(sink_ref, q_ref, kvc_ref, kvp_ref, bias_ref, o_ref):
    first_tile = pl.program_id(1) == 0
    pair = 2 * HEAD_DIM
    nblk = ATTN_TILE // WINDOW
    group = ATTN_HEADS // ATTN_KV_HEADS
    scale = HEAD_DIM ** -0.5
    low = lax.broadcasted_iota(jnp.int32, (WINDOW, pair), 1) < HEAD_DIM
    ones_bd = jnp.concatenate([jnp.where(low, 1.0, 0.0), jnp.where(low, 0.0, 1.0)], axis=0).astype(BF16)
    prev_cols = lax.broadcasted_iota(jnp.int32, (WINDOW, 4 * WINDOW), 1) < 2 * WINDOW

    def expand(x, xr, kvh):
        first, second = (x, xr) if kvh == 0 else (xr, x)
        return jnp.concatenate([jnp.where(low, first, 0.0), jnp.where(low, 0.0, second)], axis=0).astype(BF16)

    kblk, vblk = [], []
    for r in range(nblk + 1):
        kv = (kvp_ref[...] if r == 0 else kvc_ref[(r - 1) * WINDOW:r * WINDOW, :]).astype(F32)
        k2, v2 = kv[:, :pair] * scale, kv[:, pair:]
        k2r, v2r = pltpu.roll(k2, HEAD_DIM, 1), pltpu.roll(v2, HEAD_DIM, 1)
        kblk.append([expand(k2, k2r, kvh) for kvh in range(ATTN_KV_HEADS)])
        vblk.append([jnp.concatenate([expand(v2, v2r, kvh), ones_bd], axis=1) for kvh in range(ATTN_KV_HEADS)])

    kb = [[jnp.concatenate([kblk[r][kvh], kblk[r + 1][kvh]], axis=0) for kvh in range(ATTN_KV_HEADS)]
          for r in range(nblk)]
    vb = [[jnp.concatenate([vblk[r][kvh], vblk[r + 1][kvh]], axis=0) for kvh in range(ATTN_KV_HEADS)]
          for r in range(nblk)]

    def scores(r, j):
        rows, cols = slice(r * WINDOW, (r + 1) * WINDOW), slice(j * pair, (j + 1) * pair)
        s = _dot_nt(q_ref[rows, cols], kb[r][(2 * j) // group]) + bias_ref[j]
        if r == 0:
            s = jnp.where(first_tile & prev_cols, MASK_VALUE, s)
        return s

    def weigh(r, j, s):
        rows, cols = slice(r * WINDOW, (r + 1) * WINDOW), slice(j * pair, (j + 1) * pair)
        sa0, sb0, sa1, sb1 = (s[:, i * WINDOW:(i + 1) * WINDOW] for i in range(4))
        sink_a, sink_b = sink_ref[2 * j], sink_ref[2 * j + 1]
        ma = jnp.maximum(jnp.max(jnp.maximum(sa0, sa1), axis=-1, keepdims=True), sink_a)
        mb = jnp.maximum(jnp.max(jnp.maximum(sb0, sb1), axis=-1, keepdims=True), sink_b)
        e = jnp.concatenate([jnp.exp(sa0 - ma), jnp.exp(sb0 - mb), jnp.exp(sa1 - ma), jnp.exp(sb1 - mb)],
                            axis=1).astype(BF16)
        ov = _dot(e, vb[r][(2 * j) // group])
        sink_term = jnp.where(low, jnp.exp(sink_a - ma), jnp.exp(sink_b - mb))
        o_ref[rows, cols] = (ov[:, :pair] / (ov[:, pair:] + sink_term)).astype(o_ref.dtype)

    for r in range(nblk):
        for j in range(ATTN_HEADS // 2):
            weigh(r, j, scores(r, j))


def _attention_core(qkv, sinks, batch, seq):
    n = batch * seq
    nt = seq // ATTN_TILE
    blocks_per_tile = ATTN_TILE // WINDOW
    q_cols = ATTN_HEADS * HEAD_DIM
    kv_cols = 2 * ATTN_KV_HEADS * HEAD_DIM
    kv_col_block = q_cols // kv_cols
    bias = _attn_bias()

    def prev_block(b, i):
        return ((b * nt + i) * blocks_per_tile - jnp.minimum(i, 1), kv_col_block)

    return pl.pallas_call(
        _attn_kernel,
        grid=(batch, nt),
        in_specs=[
            pl.BlockSpec(memory_space=pltpu.SMEM),
            pl.BlockSpec((ATTN_TILE, q_cols), lambda b, i: (b * nt + i, 0)),
            pl.BlockSpec((ATTN_TILE, kv_cols), lambda b, i: (b * nt + i, kv_col_block)),
            pl.BlockSpec((WINDOW, kv_cols), prev_block),
            _resident(bias.shape),
        ],
        out_specs=pl.BlockSpec((ATTN_TILE, q_cols), lambda b, i: (b * nt + i, 0)),
        out_shape=jax.ShapeDtypeStruct((n, q_cols), BF16),
        compiler_params=_params("parallel", "parallel"),
        name="attn_core",
    )(sinks, qkv, qkv, qkv, bias)


def _conv_kernel(u_ref, w_ref, bdw_ref, lng_ref, lnb_ref, o_ref, buf_ref, wb_ref, c_ref):
    t = pl.program_id(1)
    tile = u_ref.shape[0]

    @pl.when((pl.program_id(0) == 0) & (t == 0))
    def _():
        for j in range(CONV_WIDTH):
            wb_ref[j] = jnp.broadcast_to(w_ref[j:j + 1, :], (SUBLANES, D_MODEL))

    @pl.when(t == 0)
    def _():
        buf_ref[0, 0:CONV_HALO, :] = jnp.zeros((CONV_HALO, D_MODEL), F32)

    @pl.when(t > 0)
    def _():
        buf_ref[0, 0:CONV_HALO, :] = buf_ref[0, tile:tile + CONV_HALO, :]

    buf_ref[0, CONV_HALO:, :] = u_ref[...]
    shifted_rows = tile + CONV_HALO - SUBLANES
    for r in range(1, SUBLANES):
        buf_ref[r, 0:shifted_rows, :] = buf_ref[0, r:r + shifted_rows, :]

    first_tap_row = CONV_HALO - (CONV_WIDTH - 1)
    groups = CONV_ROWS // SUBLANES

    def conv_rows(i, carry):
        base = pl.multiple_of(i * CONV_ROWS, CONV_ROWS)
        acc = None
        for l in range(0, D_MODEL, CONV_LANES):
            lanes = slice(l, l + CONV_LANES)
            start = bdw_ref[:, lanes]
            if acc is not None:
                start = start + _zero_like(acc[0])
            acc = jnp.broadcast_to(start, (groups, SUBLANES, CONV_LANES))
            for r in range(SUBLANES):
                taps = [j for j in range(CONV_WIDTH) if (first_tap_row + j) % SUBLANES == r]
                a0 = (first_tap_row + taps[0]) // SUBLANES
                span = (first_tap_row + taps[-1]) // SUBLANES - a0 + groups
                big = buf_ref[r, pl.ds(base + a0 * SUBLANES, span * SUBLANES), lanes]
                big = big.reshape(span, SUBLANES, CONV_LANES)
                for j in taps:
                    a = (first_tap_row + j) // SUBLANES - a0
                    acc = acc + wb_ref[j, :, lanes] * big[a:a + groups]
            c_ref[pl.ds(base, CONV_ROWS), lanes] = acc.reshape(CONV_ROWS, CONV_LANES)
        return carry

    lax.fori_loop(0, tile // CONV_ROWS, conv_rows, 0)

    def norm_rows(i, carry):
        base = pl.multiple_of(i * NORM_ROWS, NORM_ROWS)
        c = c_ref[pl.ds(base, NORM_ROWS), :]
        mu = jnp.mean(c, axis=-1, keepdims=True)
        cen = c - mu
        var = jnp.mean(cen * cen, axis=-1, keepdims=True)
        y = cen * lax.rsqrt(var + NORM_EPS) * lng_ref[...] + lnb_ref[...]
        o_ref[pl.ds(base, NORM_ROWS), :] = _silu(y).astype(o_ref.dtype)
        return carry

    lax.fori_loop(0, tile // NORM_ROWS, norm_rows, 0)


def _conv_core(u, w_dw, b_dw, ln_g, ln_b, batch, seq):
    n, d = u.shape
    nt = seq // CONV_TILE
    row = lambda b, t: (b * nt + t, 0)
    return pl.pallas_call(
        _conv_kernel,
        grid=(batch, nt),
        in_specs=[pl.BlockSpec((CONV_TILE, d), row), _resident((CONV_WIDTH, d)),
                  _resident((1, d)), _resident((1, d)), _resident((1, d))],
        out_specs=pl.BlockSpec((CONV_TILE, d), row),
        out_shape=jax.ShapeDtypeStruct((n, d), BF16),
        scratch_shapes=[pltpu.VMEM((SUBLANES, CONV_TILE + CONV_HALO, d), F32),
                        pltpu.VMEM((CONV_WIDTH, SUBLANES, d), F32),
                        pltpu.VMEM((CONV_TILE, d), F32)],
        compiler_params=_params("arbitrary", "arbitrary"),
        name="conv_core",
    )(u, w_dw, b_dw, ln_g, ln_b)


def _hgrn_constants():
    c = HGRN_CHUNK
    t = jnp.arange(c)[:, None]
    s = jnp.arange(c)[None, :]
    tri = (s <= t).astype(BF16)
    masks = []
    for hs in HGRN_LEVELS[1:]:
        same = (t // (2 * hs)) == (s // (2 * hs))
        masks.append(same & ((t // hs) % 2 == 1) & ((s // hs) % 2 == 0))
    masks.append(t == s)
    return tri, jnp.stack(masks).astype(F32)


def _hgrn_kernel(x_ref, lbp_ref, ng_ref, tri_ref, mask_ref, o_ref, q_s, k_s, b_s, st_ref, *, layer):
    chunk = HGRN_CHUNK
    blocks = chunk // SUBLANES

    @pl.when(pl.program_id(1) == 0)
    def _():
        st_ref[...] = jnp.zeros(st_ref.shape, F32)

    p = lbp_ref[...]
    e = jnp.exp(p - jnp.max(p, axis=0, keepdims=True))
    sm = e / jnp.sum(e, axis=0, keepdims=True)
    lb = jnp.zeros((1, D_MODEL), F32)
    for l in range(1, layer + 1):
        lb = lb + sm[l:l + 1, :]

    def prepare(c):
        rows = slice(c * chunk, (c + 1) * chunk)
        q_s[c] = _silu(x_ref[rows, 0:D_MODEL]) * (HGRN_DK ** -0.5)
        f = lb + (1.0 - lb) * _sigmoid(x_ref[rows, D_MODEL:2 * D_MODEL])
        k_s[c] = 1.0 - f
        logf = jnp.log(f)
        g1 = logf.astype(BF16)
        r1 = logf - g1.astype(F32)
        g2 = r1.astype(BF16)
        g3 = (r1 - g2.astype(F32)).astype(BF16)
        tri = tri_ref[...]
        b_s[c] = (_dot(tri, g1) + _dot(tri, g2) + _dot(tri, g3)) * LOG2E

    row_in_blk = lax.broadcasted_iota(jnp.int32, (blocks, SUBLANES, HGRN_DK), 1)
    upper_of_8 = row_in_blk >= 4
    sign4 = jnp.where(upper_of_8, 1.0, -1.0)
    sign2 = jnp.where(row_in_blk % 4 >= 2, 1.0, -1.0)

    def scores(c, hd):
        sl = slice(hd * HGRN_DK, (hd + 1) * HGRN_DK)
        qh, kh, bh = q_s[c, :, sl], k_s[c, :, sl], b_s[c, :, sl]
        kb = kh.astype(BF16)
        state = st_ref[hd]

        o = _dot_nt((qh * jnp.exp2(bh)).astype(BF16), state.astype(BF16))

        a = jnp.zeros((chunk, chunk), F32)
        for lvl, hs in enumerate(HGRN_LEVELS[:4]):
            q_parts, k_parts = [], []
            zeros = jnp.zeros((hs, HGRN_DK), F32)
            for s0 in range(0, chunk, 2 * hs):
                pivot = bh[s0 + hs - 1:s0 + hs, :]
                lo, up = slice(s0, s0 + hs), slice(s0 + hs, s0 + 2 * hs)
                q_parts += [zeros, qh[up] * jnp.exp2(bh[up] - pivot)]
                k_parts += [kh[lo] * jnp.exp2(pivot - bh[lo]), zeros]
            al = _dot_nt(jnp.concatenate(q_parts, axis=0).astype(BF16),
                         jnp.concatenate(k_parts, axis=0).astype(BF16))
            a = a + (al if lvl == 0 else al * mask_ref[lvl - 1])

        b3 = bh.reshape(blocks, SUBLANES, HGRN_DK)
        q3 = qh.reshape(blocks, SUBLANES, HGRN_DK)
        k3 = kh.reshape(blocks, SUBLANES, HGRN_DK)
        pivot4 = b3[:, 3:4, :]
        pivot2 = jnp.where(upper_of_8, b3[:, 5:6, :], b3[:, 1:2, :])
        for lvl, w3 in ((4, jnp.exp2((b3 - pivot4) * sign4)), (5, jnp.exp2((b3 - pivot2) * sign2))):
            al = _dot_nt((q3 * w3).reshape(chunk, HGRN_DK).astype(BF16),
                         (k3 * w3).reshape(chunk, HGRN_DK).astype(BF16))
            a = a + al * mask_ref[lvl - 1]
        a = a + _dot_nt((qh * (1.0 - kh)).astype(BF16), kb) * mask_ref[5]
        a = a + _dot_nt(qh.astype(BF16), kb) * mask_ref[6]
        return o, a, state

    def mix(c, hd, o, a, state):
        sl = slice(hd * HGRN_DK, (hd + 1) * HGRN_DK)
        rows = slice(c * chunk, (c + 1) * chunk)
        kh, bh = k_s[c, :, sl], b_s[c, :, sl]
        vh = x_ref[rows, 2 * D_MODEL + hd * HGRN_DK:2 * D_MODEL + (hd + 1) * HGRN_DK]
        b_last = bh[chunk - 1:chunk, :]
        o = o + _dot(a.astype(BF16), vh.astype(BF16))
        k_dec = (kh * jnp.exp2(b_last - bh)).astype(BF16)
        st_ref[hd] = state * jnp.exp2(b_last) + _dot(vh.T.astype(BF16), k_dec)
        return o

    def readout(c, hd, o):
        sl = slice(hd * HGRN_DK, (hd + 1) * HGRN_DK)
        rows = slice(c * chunk, (c + 1) * chunk)
        gr = x_ref[rows, 3 * D_MODEL + hd * HGRN_DK:3 * D_MODEL + (hd + 1) * HGRN_DK]
        on = o * lax.rsqrt(jnp.mean(o * o, axis=-1, keepdims=True) + NORM_EPS)
        o_ref[rows, sl] = (on * ng_ref[:, sl] * _silu(gr)).astype(o_ref.dtype)

    items = [(c, hd) for c in range(HGRN_STEP_CHUNKS) for hd in range(HGRN_HEADS)]
    prepare(0)
    stage1, stage2 = {}, {}
    for i in range(len(items) + 2 * HEAD_SKEW):
        if i < len(items):
            stage1[i] = scores(*items[i])
        if 0 <= i - HEAD_SKEW < len(items):
            stage2[i - HEAD_SKEW] = mix(*items[i - HEAD_SKEW], *stage1.pop(i - HEAD_SKEW))
        if 0 <= i - 2 * HEAD_SKEW < len(items):
            readout(*items[i - 2 * HEAD_SKEW], stage2.pop(i - 2 * HEAD_SKEW))
        if 1 <= i < HGRN_STEP_CHUNKS:
            prepare(i)


def _hgrn_core(qfig, lower_bounds, norm_g, layer, batch, seq):
    n = qfig.shape[0]
    rows = HGRN_STEP_CHUNKS * HGRN_CHUNK
    nc = seq // rows
    tri, masks = _hgrn_constants()
    return pl.pallas_call(
        functools.partial(_hgrn_kernel, layer=layer),
        grid=(batch, nc),
        in_specs=[pl.BlockSpec((rows, 4 * D_MODEL), lambda b, c: (b * nc + c, 0)),
                  _resident(lower_bounds.shape), _resident((1, D_MODEL)),
                  _resident(tri.shape), _resident(masks.shape)],
        out_specs=pl.BlockSpec((rows, D_MODEL), lambda b, c: (b * nc + c, 0)),
        out_shape=jax.ShapeDtypeStruct((n, D_MODEL), BF16),
        scratch_shapes=[pltpu.VMEM((HGRN_STEP_CHUNKS, HGRN_CHUNK, D_MODEL), F32)] * 3
                       + [pltpu.VMEM((HGRN_HEADS, HGRN_DK, HGRN_DK), F32)],
        compiler_params=_params("arbitrary", "arbitrary"),
        name="hgrn_core",
    )(qfig, lower_bounds, norm_g, tri, masks)


def _resid_mlp_kernel(h_ref, y_ref, wo_ref, bo_ref, g_ref, w1_ref, w2_ref, gf_ref, o_ref, *,
                      has_bias, final_norm):
    h2 = h_ref[...] + _dot(y_ref[...], wo_ref[...])
    if has_bias:
        h2 = h2 + bo_ref[...]
    hn = _rms_norm(h2, g_ref[...]).astype(BF16)
    acc = h2
    for c in range(0, D_FF, FF_CHUNK):
        up = jnp.maximum(_dot(hn, w1_ref[:, c:c + FF_CHUNK]), 0.0)
        acc = acc + _dot((up * up).astype(BF16), w2_ref[c:c + FF_CHUNK, :])
    if final_norm:
        acc = _rms_norm(acc, gf_ref[...])
    o_ref[...] = acc


def _resid_mlp(h, y, wo_stack, wo_layer, bo, g, w1_stack, w2_stack, layer, gf, has_bias, final_norm):
    n, d = h.shape
    row = lambda i: (i, 0)
    return pl.pallas_call(
        functools.partial(_resid_mlp_kernel, has_bias=has_bias, final_norm=final_norm),
        grid=(n // ROW_TILE,),
        in_specs=[pl.BlockSpec((ROW_TILE, d), row), pl.BlockSpec((ROW_TILE, d), row),
                  _resident(wo_stack.shape, wo_layer), _resident((1, d)), _resident((1, d)),
                  _resident(w1_stack.shape, layer), _resident(w2_stack.shape, layer), _resident((1, d))],
        out_specs=pl.BlockSpec((ROW_TILE, d), row),
        out_shape=jax.ShapeDtypeStruct((n, d), F32),
        compiler_params=_params("parallel"),
        name="resid_mlp",
    )(h, y, wo_stack, bo, g, w1_stack, w2_stack, gf)


def kernel(x, attn_w_qkv, attn_sinks, attn_w_o, conv_w_pw1, conv_b_pw1, conv_w_dw, conv_b_dw, conv_ln_g,
           conv_ln_b, conv_w_pw2, conv_b_pw2, hgrn_w_qfig, hgrn_lower_bounds, hgrn_norm_g, hgrn_w_o,
           norm_mixer, norm_mlp, mlp_w1, mlp_w2, final_norm):
    batch, seq, d = x.shape
    h = x.reshape(batch * seq, d)
    row = lambda v: v.reshape(1, -1)
    zero_bias = jnp.zeros((1, d), F32)
    w_qkv, w_ao = attn_w_qkv.astype(BF16), attn_w_o.astype(BF16)
    w_pw1, w_pw2 = conv_w_pw1.astype(BF16), conv_w_pw2.astype(BF16)
    w_qfig, w_ho = hgrn_w_qfig.astype(BF16), hgrn_w_o.astype(BF16)
    w1, w2 = mlp_w1.astype(BF16), mlp_w2.astype(BF16)
    for i in range(DEPTH):
        kind, j = i % N_MIXERS, i // N_MIXERS
        g_mix = row(norm_mixer[i])
        bias, has_bias = zero_bias, False
        if kind == 0:
            qkv = _norm_proj(h, g_mix, w_qkv, j, BF16)
            y = _attention_core(qkv, attn_sinks[j], batch, seq)
            wo = w_ao
        elif kind == 1:
            u = _norm_glu(h, g_mix, w_pw1, j, row(conv_b_pw1[j]))
            y = _conv_core(u, conv_w_dw[j], row(conv_b_dw[j]), row(conv_ln_g[j]), row(conv_ln_b[j]), batch, seq)
            wo, bias, has_bias = w_pw2, row(conv_b_pw2[j]), True
        else:
            qfig = _norm_proj(h, g_mix, w_qfig, j, F32)
            y = _hgrn_core(qfig, hgrn_lower_bounds, row(hgrn_norm_g[j]), i, batch, seq)
            wo = w_ho
        h = _resid_mlp(h, y, wo, j, bias, row(norm_mlp[i]), w1, w2, i, row(final_norm),
                       has_bias, i == DEPTH - 1)
    return h.reshape(batch, seq, d)
```

```python
import functools
import math

import jax
import jax.numpy as jnp
from jax import lax
from jax.experimental import pallas as pl
from jax.experimental.pallas import tpu as pltpu

F32 = jnp.float32
BF16 = jnp.bfloat16

D_MODEL = 1024
DEPTH = 4
N_MIXERS = 3
ATTN_HEADS = 16
ATTN_KV_HEADS = 2
HEAD_DIM = 64
WINDOW = 128
QKV_DIM = (ATTN_HEADS + 2 * ATTN_KV_HEADS) * HEAD_DIM
CONV_WIDTH = 31
HGRN_HEADS = 8
HGRN_DK = 128
D_FF = 4 * D_MODEL
NORM_EPS = 1e-6
MASK_VALUE = -1e30
LOG2E = math.log2(math.e)

V7X_VMEM_BYTES = 64 * 1024 * 1024
VMEM_LIMIT_BYTES = V7X_VMEM_BYTES - 8 * 1024 * 1024
SUBLANES = 8
LANES = 128

ROW_TILE = 1024
FF_CHUNK = 1024
ATTN_TILE = 1024
CONV_TILE = 512
CONV_HALO = 32
CONV_ROWS = 32
CONV_LANES = 128
NORM_ROWS = 256
HGRN_CHUNK = 128
HGRN_STEP_CHUNKS = 4
HGRN_LEVELS = (64, 32, 16, 8, 4, 2, 1)
HEAD_SKEW = 4


def _params(*semantics):
    return pltpu.CompilerParams(dimension_semantics=semantics, vmem_limit_bytes=VMEM_LIMIT_BYTES)


def _resident(shape, lead=None):
    if lead is None:
        return pl.BlockSpec(shape, lambda *_: (0,) * len(shape), pipeline_mode=pl.Buffered(1))
    return pl.BlockSpec((None,) + tuple(shape[1:]), lambda *_: (lead,) + (0,) * (len(shape) - 1),
                        pipeline_mode=pl.Buffered(1))


def _rms_norm(x, g):
    ms = jnp.mean(x * x, axis=-1, keepdims=True)
    return x * lax.rsqrt(ms + NORM_EPS) * g


def _sigmoid(x):
    return 0.5 * jnp.tanh(0.5 * x) + 0.5


def _silu(x):
    h = 0.5 * x
    return h + h * jnp.tanh(h)


def _zero_like(x):
    bits = lax.bitcast_convert_type(x, jnp.uint32)
    sixteen = jnp.uint32(16)
    return lax.bitcast_convert_type((bits >> sixteen) >> sixteen, F32)


N_CAST = 3


def _cast_specs(weights, n_steps, step_index):
    in_specs, out_specs, out_shapes = [], [], []
    for stack, layer in weights:
        _, rows, cols = stack.shape
        part = rows // n_steps
        in_specs.append(pl.BlockSpec((None, part, cols), lambda *g, layer=layer: (layer, step_index(*g), 0)))
        out_specs.append(pl.BlockSpec((part, cols), lambda *g: (step_index(*g), 0)))
        out_shapes.append(jax.ShapeDtypeStruct((rows, cols), BF16))
    return in_specs, out_specs, out_shapes


def _cast_weights(src_refs, dst_refs):
    for src, dst in zip(src_refs, dst_refs):
        dst[...] = src[...].astype(BF16)


def _dot(a, b):
    return jnp.dot(a, b, preferred_element_type=F32)


def _dot_nt(a, b):
    return lax.dot_general(a, b, (((1,), (1,)), ((), ())), preferred_element_type=F32)


def _norm_proj_kernel(x_ref, g_ref, w_ref, o_ref):
    hn = _rms_norm(x_ref[...], g_ref[...]).astype(BF16)
    n_out = o_ref.shape[1]
    chunk = min(n_out, FF_CHUNK)
    for c in range(0, n_out, chunk):
        width = min(chunk, n_out - c)
        o_ref[:, c:c + width] = _dot(hn, w_ref[:, c:c + width]).astype(o_ref.dtype)


def _norm_glu_kernel(x_ref, g_ref, w_ref, b_ref, o_ref):
    hn = _rms_norm(x_ref[...], g_ref[...]).astype(BF16)
    a = _dot(hn, w_ref[:, :D_MODEL]) + b_ref[:, :D_MODEL]
    gate = _dot(hn, w_ref[:, D_MODEL:]) + b_ref[:, D_MODEL:]
    o_ref[...] = a * _sigmoid(gate)


def _norm_proj(h, g, w_stack, layer, out_dtype):
    n, d = h.shape
    n_out = w_stack.shape[2]
    return pl.pallas_call(
        _norm_proj_kernel,
        grid=(n // ROW_TILE,),
        in_specs=[pl.BlockSpec((ROW_TILE, d), lambda i: (i, 0)), _resident((1, d)),
                  _resident(w_stack.shape, layer)],
        out_specs=pl.BlockSpec((ROW_TILE, n_out), lambda i: (i, 0)),
        out_shape=jax.ShapeDtypeStruct((n, n_out), out_dtype),
        compiler_params=_params("parallel"),
        name="norm_proj",
    )(h, g, w_stack)


def _norm_glu(h, g, w_stack, layer, b):
    n, d = h.shape
    return pl.pallas_call(
        _norm_glu_kernel,
        grid=(n // ROW_TILE,),
        in_specs=[pl.BlockSpec((ROW_TILE, d), lambda i: (i, 0)), _resident((1, d)),
                  _resident(w_stack.shape, layer), _resident((1, 2 * d))],
        out_specs=pl.BlockSpec((ROW_TILE, d), lambda i: (i, 0)),
        out_shape=jax.ShapeDtypeStruct((n, d), F32),
        compiler_params=_params("parallel"),
        name="norm_glu",
    )(h, g, w_stack, b)


def _attn_bias():
    qi = jnp.arange(WINDOW)[:, None]
    col = jnp.arange(4 * WINDOW)[None, :]
    si = (col // (2 * WINDOW)) * WINDOW + col % WINDOW
    second = (col // WINDOW) % 2
    dist = WINDOW + qi - si
    valid = (dist >= 0) & (dist < WINDOW)
    head = 2 * jnp.arange(ATTN_HEADS // 2)[:, None, None] + second[None]
    slopes = jnp.exp2(-8.0 * (head + 1).astype(F32) / ATTN_HEADS)
    return jnp.where(valid[None], -(slopes * dist[None].astype(F32)), MASK_VALUE)


def _attn_kernel(sink_ref, q_ref, kvc_ref, kvp_ref, bias_ref, *refs):
    o_ref = refs[N_CAST]
    _cast_weights(refs[:N_CAST], refs[N_CAST + 1:])
    first_tile = pl.program_id(1) == 0
    pair = 2 * HEAD_DIM
    nblk = ATTN_TILE // WINDOW
    group = ATTN_HEADS // ATTN_KV_HEADS
    scale = HEAD_DIM ** -0.5
    low = lax.broadcasted_iota(jnp.int32, (WINDOW, pair), 1) < HEAD_DIM
    ones_bd = jnp.concatenate([jnp.where(low, 1.0, 0.0), jnp.where(low, 0.0, 1.0)], axis=0).astype(BF16)
    prev_cols = lax.broadcasted_iota(jnp.int32, (WINDOW, 4 * WINDOW), 1) < 2 * WINDOW

    def expand(x, xr, kvh):
        first, second = (x, xr) if kvh == 0 else (xr, x)
        return jnp.concatenate([jnp.where(low, first, 0.0), jnp.where(low, 0.0, second)], axis=0).astype(BF16)

    kblk, vblk = [], []
    for r in range(nblk + 1):
        kv = (kvp_ref[...] if r == 0 else kvc_ref[(r - 1) * WINDOW:r * WINDOW, :]).astype(F32)
        k2, v2 = kv[:, :pair] * scale, kv[:, pair:]
        k2r, v2r = pltpu.roll(k2, HEAD_DIM, 1), pltpu.roll(v2, HEAD_DIM, 1)
        kblk.append([expand(k2, k2r, kvh) for kvh in range(ATTN_KV_HEADS)])
        vblk.append([jnp.concatenate([expand(v2, v2r, kvh), ones_bd], axis=1) for kvh in range(ATTN_KV_HEADS)])

    kb = [[jnp.concatenate([kblk[r][kvh], kblk[r + 1][kvh]], axis=0) for kvh in range(ATTN_KV_HEADS)]
          for r in range(nblk)]
    vb = [[jnp.concatenate([vblk[r][kvh], vblk[r + 1][kvh]], axis=0) for kvh in range(ATTN_KV_HEADS)]
          for r in range(nblk)]

    def scores(r, j):
        rows, cols = slice(r * WINDOW, (r + 1) * WINDOW), slice(j * pair, (j + 1) * pair)
        s = _dot_nt(q_ref[rows, cols], kb[r][(2 * j) // group]) + bias_ref[j]
        if r == 0:
            s = jnp.where(first_tile & prev_cols, MASK_VALUE, s)
        return s

    def weigh(r, j, s):
        rows, cols = slice(r * WINDOW, (r + 1) * WINDOW), slice(j * pair, (j + 1) * pair)
        sa0, sb0, sa1, sb1 = (s[:, i * WINDOW:(i + 1) * WINDOW] for i in range(4))
        sink_a, sink_b = sink_ref[2 * j], sink_ref[2 * j + 1]
        ma = jnp.maximum(jnp.max(jnp.maximum(sa0, sa1), axis=-1, keepdims=True), sink_a)
        mb = jnp.maximum(jnp.max(jnp.maximum(sb0, sb1), axis=-1, keepdims=True), sink_b)
        e = jnp.concatenate([jnp.exp(sa0 - ma), jnp.exp(sb0 - mb), jnp.exp(sa1 - ma), jnp.exp(sb1 - mb)],
                            axis=1).astype(BF16)
        ov = _dot(e, vb[r][(2 * j) // group])
        sink_term = jnp.where(low, jnp.exp(sink_a - ma), jnp.exp(sink_b - mb))
        o_ref[rows, cols] = (ov[:, :pair] / (ov[:, pair:] + sink_term)).astype(o_ref.dtype)

    for r in range(nblk):
        for j in range(ATTN_HEADS // 2):
            weigh(r, j, scores(r, j))


def _attention_core(qkv, sinks, weights, batch, seq):
    n = batch * seq
    nt = seq // ATTN_TILE
    blocks_per_tile = ATTN_TILE // WINDOW
    q_cols = ATTN_HEADS * HEAD_DIM
    kv_cols = 2 * ATTN_KV_HEADS * HEAD_DIM
    kv_col_block = q_cols // kv_cols
    bias = _attn_bias()
    cast_in, cast_out, cast_shapes = _cast_specs(weights, batch * nt, lambda b, i: b * nt + i)

    def prev_block(b, i):
        return ((b * nt + i) * blocks_per_tile - jnp.minimum(i, 1), kv_col_block)

    return pl.pallas_call(
        _attn_kernel,
        grid=(batch, nt),
        in_specs=[
            pl.BlockSpec(memory_space=pltpu.SMEM),
            pl.BlockSpec((ATTN_TILE, q_cols), lambda b, i: (b * nt + i, 0)),
            pl.BlockSpec((ATTN_TILE, kv_cols), lambda b, i: (b * nt + i, kv_col_block)),
            pl.BlockSpec((WINDOW, kv_cols), prev_block),
            _resident(bias.shape),
        ] + cast_in,
        out_specs=[pl.BlockSpec((ATTN_TILE, q_cols), lambda b, i: (b * nt + i, 0))] + cast_out,
        out_shape=[jax.ShapeDtypeStruct((n, q_cols), BF16)] + cast_shapes,
        compiler_params=_params("parallel", "parallel"),
        name="attn_core",
    )(sinks, qkv, qkv, qkv, bias, *[stack for stack, _ in weights])


def _conv_kernel(u_ref, w_ref, bdw_ref, lng_ref, lnb_ref, *refs):
    o_ref = refs[N_CAST]
    buf_ref, wb_ref, c_ref = refs[2 * N_CAST + 1:]
    _cast_weights(refs[:N_CAST], refs[N_CAST + 1:2 * N_CAST + 1])
    t = pl.program_id(1)
    tile = u_ref.shape[0]

    @pl.when((pl.program_id(0) == 0) & (t == 0))
    def _():
        for j in range(CONV_WIDTH):
            wb_ref[j] = jnp.broadcast_to(w_ref[j:j + 1, :], (SUBLANES, D_MODEL))

    @pl.when(t == 0)
    def _():
        buf_ref[0, 0:CONV_HALO, :] = jnp.zeros((CONV_HALO, D_MODEL), F32)

    @pl.when(t > 0)
    def _():
        buf_ref[0, 0:CONV_HALO, :] = buf_ref[0, tile:tile + CONV_HALO, :]

    buf_ref[0, CONV_HALO:, :] = u_ref[...]
    shifted_rows = tile + CONV_HALO - SUBLANES
    for r in range(1, SUBLANES):
        buf_ref[r, 0:shifted_rows, :] = buf_ref[0, r:r + shifted_rows, :]

    first_tap_row = CONV_HALO - (CONV_WIDTH - 1)
    groups = CONV_ROWS // SUBLANES

    def conv_rows(i, carry):
        base = pl.multiple_of(i * CONV_ROWS, CONV_ROWS)
        acc = None
        for l in range(0, D_MODEL, CONV_LANES):
            lanes = slice(l, l + CONV_LANES)
            start = bdw_ref[:, lanes]
            if acc is not None:
                start = start + _zero_like(acc[0])
            acc = jnp.broadcast_to(start, (groups, SUBLANES, CONV_LANES))
            for r in range(SUBLANES):
                taps = [j for j in range(CONV_WIDTH) if (first_tap_row + j) % SUBLANES == r]
                a0 = (first_tap_row + taps[0]) // SUBLANES
                span = (first_tap_row + taps[-1]) // SUBLANES - a0 + groups
                big = buf_ref[r, pl.ds(base + a0 * SUBLANES, span * SUBLANES), lanes]
                big = big.reshape(span, SUBLANES, CONV_LANES)
                for j in taps:
                    a = (first_tap_row + j) // SUBLANES - a0
                    acc = acc + wb_ref[j, :, lanes] * big[a:a + groups]
            c_ref[pl.ds(base, CONV_ROWS), lanes] = acc.reshape(CONV_ROWS, CONV_LANES)
        return carry

    lax.fori_loop(0, tile // CONV_ROWS, conv_rows, 0)

    def norm_rows(i, carry):
        base = pl.multiple_of(i * NORM_ROWS, NORM_ROWS)
        c = c_ref[pl.ds(base, NORM_ROWS), :]
        mu = jnp.mean(c, axis=-1, keepdims=True)
        cen = c - mu
        var = jnp.mean(cen * cen, axis=-1, keepdims=True)
        y = cen * lax.rsqrt(var + NORM_EPS) * lng_ref[...] + lnb_ref[...]
        o_ref[pl.ds(base, NORM_ROWS), :] = _silu(y).astype(o_ref.dtype)
        return carry

    lax.fori_loop(0, tile // NORM_ROWS, norm_rows, 0)


def _conv_core(u, w_dw, b_dw, ln_g, ln_b, weights, batch, seq):
    n, d = u.shape
    nt = seq // CONV_TILE
    row = lambda b, t: (b * nt + t, 0)
    cast_in, cast_out, cast_shapes = _cast_specs(weights, batch * nt, lambda b, t: b * nt + t)
    return pl.pallas_call(
        _conv_kernel,
        grid=(batch, nt),
        in_specs=[pl.BlockSpec((CONV_TILE, d), row), _resident((CONV_WIDTH, d)),
                  _resident((1, d)), _resident((1, d)), _resident((1, d))] + cast_in,
        out_specs=[pl.BlockSpec((CONV_TILE, d), row)] + cast_out,
        out_shape=[jax.ShapeDtypeStruct((n, d), BF16)] + cast_shapes,
        scratch_shapes=[pltpu.VMEM((SUBLANES, CONV_TILE + CONV_HALO, d), F32),
                        pltpu.VMEM((CONV_WIDTH, SUBLANES, d), F32),
                        pltpu.VMEM((CONV_TILE, d), F32)],
        compiler_params=_params("arbitrary", "arbitrary"),
        name="conv_core",
    )(u, w_dw, b_dw, ln_g, ln_b, *[stack for stack, _ in weights])


def _hgrn_constants():
    c = HGRN_CHUNK
    t = jnp.arange(c)[:, None]
    s = jnp.arange(c)[None, :]
    tri = (s <= t).astype(BF16)
    masks = []
    for hs in HGRN_LEVELS[1:]:
        same = (t // (2 * hs)) == (s // (2 * hs))
        masks.append(same & ((t // hs) % 2 == 1) & ((s // hs) % 2 == 0))
    masks.append(t == s)
    return tri, jnp.stack(masks).astype(F32)


def _hgrn_kernel(x_ref, lbp_ref, ng_ref, tri_ref, mask_ref, *refs, layer):
    o_ref = refs[N_CAST]
    q_s, k_s, b_s, st_ref = refs[2 * N_CAST + 1:]
    _cast_weights(refs[:N_CAST], refs[N_CAST + 1:2 * N_CAST + 1])
    chunk = HGRN_CHUNK
    blocks = chunk // SUBLANES

    @pl.when(pl.program_id(1) == 0)
    def _():
        st_ref[...] = jnp.zeros(st_ref.shape, F32)

    p = lbp_ref[...]
    e = jnp.exp(p - jnp.max(p, axis=0, keepdims=True))
    sm = e / jnp.sum(e, axis=0, keepdims=True)
    lb = jnp.zeros((1, D_MODEL), F32)
    for l in range(1, layer + 1):
        lb = lb + sm[l:l + 1, :]

    def prepare(c):
        rows = slice(c * chunk, (c + 1) * chunk)
        q_s[c] = _silu(x_ref[rows, 0:D_MODEL]) * (HGRN_DK ** -0.5)
        f = lb + (1.0 - lb) * _sigmoid(x_ref[rows, D_MODEL:2 * D_MODEL])
        k_s[c] = 1.0 - f
        logf = jnp.log(f)
        g1 = logf.astype(BF16)
        r1 = logf - g1.astype(F32)
        g2 = r1.astype(BF16)
        g3 = (r1 - g2.astype(F32)).astype(BF16)
        tri = tri_ref[...]
        b_s[c] = (_dot(tri, g1) + _dot(tri, g2) + _dot(tri, g3)) * LOG2E

    row_in_blk = lax.broadcasted_iota(jnp.int32, (blocks, SUBLANES, HGRN_DK), 1)
    upper_of_8 = row_in_blk >= 4
    sign4 = jnp.where(upper_of_8, 1.0, -1.0)
    sign2 = jnp.where(row_in_blk % 4 >= 2, 1.0, -1.0)

    def scores(c, hd):
        sl = slice(hd * HGRN_DK, (hd + 1) * HGRN_DK)
        qh, kh, bh = q_s[c, :, sl], k_s[c, :, sl], b_s[c, :, sl]
        kb = kh.astype(BF16)
        state = st_ref[hd]

        o = _dot_nt((qh * jnp.exp2(bh)).astype(BF16), state.astype(BF16))

        a = jnp.zeros((chunk, chunk), F32)
        for lvl, hs in enumerate(HGRN_LEVELS[:4]):
            q_parts, k_parts = [], []
            zeros = jnp.zeros((hs, HGRN_DK), F32)
            for s0 in range(0, chunk, 2 * hs):
                pivot = bh[s0 + hs - 1:s0 + hs, :]
                lo, up = slice(s0, s0 + hs), slice(s0 + hs, s0 + 2 * hs)
                q_parts += [zeros, qh[up] * jnp.exp2(bh[up] - pivot)]
                k_parts += [kh[lo] * jnp.exp2(pivot - bh[lo]), zeros]
            al = _dot_nt(jnp.concatenate(q_parts, axis=0).astype(BF16),
                         jnp.concatenate(k_parts, axis=0).astype(BF16))
            a = a + (al if lvl == 0 else al * mask_ref[lvl - 1])

        b3 = bh.reshape(blocks, SUBLANES, HGRN_DK)
        q3 = qh.reshape(blocks, SUBLANES, HGRN_DK)
        k3 = kh.reshape(blocks, SUBLANES, HGRN_DK)
        pivot4 = b3[:, 3:4, :]
        pivot2 = jnp.where(upper_of_8, b3[:, 5:6, :], b3[:, 1:2, :])
        for lvl, w3 in ((4, jnp.exp2((b3 - pivot4) * sign4)), (5, jnp.exp2((b3 - pivot2) * sign2))):
            al = _dot_nt((q3 * w3).reshape(chunk, HGRN_DK).astype(BF16),
                         (k3 * w3).reshape(chunk, HGRN_DK).astype(BF16))
            a = a + al * mask_ref[lvl - 1]
        a = a + _dot_nt((qh * (1.0 - kh)).astype(BF16), kb) * mask_ref[5]
        a = a + _dot_nt(qh.astype(BF16), kb) * mask_ref[6]
        return o, a, state

    def mix(c, hd, o, a, state):
        sl = slice(hd * HGRN_DK, (hd + 1) * HGRN_DK)
        rows = slice(c * chunk, (c + 1) * chunk)
        kh, bh = k_s[c, :, sl], b_s[c, :, sl]
        vh = x_ref[rows, 2 * D_MODEL + hd * HGRN_DK:2 * D_MODEL + (hd + 1) * HGRN_DK]
        b_last = bh[chunk - 1:chunk, :]
        o = o + _dot(a.astype(BF16), vh.astype(BF16))
        k_dec = (kh * jnp.exp2(b_last - bh)).astype(BF16)
        st_ref[hd] = state * jnp.exp2(b_last) + _dot(vh.T.astype(BF16), k_dec)
        return o

    def readout(c, hd, o):
        sl = slice(hd * HGRN_DK, (hd + 1) * HGRN_DK)
        rows = slice(c * chunk, (c + 1) * chunk)
        gr = x_ref[rows, 3 * D_MODEL + hd * HGRN_DK:3 * D_MODEL + (hd + 1) * HGRN_DK]
        on = o * lax.rsqrt(jnp.mean(o * o, axis=-1, keepdims=True) + NORM_EPS)
        o_ref[rows, sl] = (on * ng_ref[:, sl] * _silu(gr)).astype(o_ref.dtype)

    items = [(c, hd) for c in range(HGRN_STEP_CHUNKS) for hd in range(HGRN_HEADS)]
    prepare(0)
    stage1, stage2 = {}, {}
    for i in range(len(items) + 2 * HEAD_SKEW):
        if i < len(items):
            stage1[i] = scores(*items[i])
        if 0 <= i - HEAD_SKEW < len(items):
            stage2[i - HEAD_SKEW] = mix(*items[i - HEAD_SKEW], *stage1.pop(i - HEAD_SKEW))
        if 0 <= i - 2 * HEAD_SKEW < len(items):
            readout(*items[i - 2 * HEAD_SKEW], stage2.pop(i - 2 * HEAD_SKEW))
        if 1 <= i < HGRN_STEP_CHUNKS:
            prepare(i)


def _hgrn_core(qfig, lower_bounds, norm_g, layer, weights, batch, seq):
    n = qfig.shape[0]
    rows = HGRN_STEP_CHUNKS * HGRN_CHUNK
    nc = seq // rows
    tri, masks = _hgrn_constants()
    cast_in, cast_out, cast_shapes = _cast_specs(weights, batch * nc, lambda b, c: b * nc + c)
    return pl.pallas_call(
        functools.partial(_hgrn_kernel, layer=layer),
        grid=(batch, nc),
        in_specs=[pl.BlockSpec((rows, 4 * D_MODEL), lambda b, c: (b * nc + c, 0)),
                  _resident(lower_bounds.shape), _resident((1, D_MODEL)),
                  _resident(tri.shape), _resident(masks.shape)] + cast_in,
        out_specs=[pl.BlockSpec((rows, D_MODEL), lambda b, c: (b * nc + c, 0))] + cast_out,
        out_shape=[jax.ShapeDtypeStruct((n, D_MODEL), BF16)] + cast_shapes,
        scratch_shapes=[pltpu.VMEM((HGRN_STEP_CHUNKS, HGRN_CHUNK, D_MODEL), F32)] * 3
                       + [pltpu.VMEM((HGRN_HEADS, HGRN_DK, HGRN_DK), F32)],
        compiler_params=_params("arbitrary", "arbitrary"),
        name="hgrn_core",
    )(qfig, lower_bounds, norm_g, tri, masks, *[stack for stack, _ in weights])


def _resid_mlp_kernel(h_ref, y_ref, wo_ref, bo_ref, g_ref, w1_ref, w2_ref, gf_ref, o_ref, *,
                      has_bias, final_norm):
    h2 = h_ref[...] + _dot(y_ref[...], wo_ref[...])
    if has_bias:
        h2 = h2 + bo_ref[...]
    hn = _rms_norm(h2, g_ref[...]).astype(BF16)
    acc = h2
    for c in range(0, D_FF, FF_CHUNK):
        up = jnp.maximum(_dot(hn, w1_ref[:, c:c + FF_CHUNK]), 0.0)
        acc = acc + _dot((up * up).astype(BF16), w2_ref[c:c + FF_CHUNK, :])
    if final_norm:
        acc = _rms_norm(acc, gf_ref[...])
    o_ref[...] = acc


def _resid_mlp(h, y, wo, bo, g, w1, w2, gf, has_bias, final_norm):
    n, d = h.shape
    row = lambda i: (i, 0)
    return pl.pallas_call(
        functools.partial(_resid_mlp_kernel, has_bias=has_bias, final_norm=final_norm),
        grid=(n // ROW_TILE,),
        in_specs=[pl.BlockSpec((ROW_TILE, d), row), pl.BlockSpec((ROW_TILE, d), row),
                  _resident(wo.shape), _resident((1, d)), _resident((1, d)),
                  _resident(w1.shape), _resident(w2.shape), _resident((1, d))],
        out_specs=pl.BlockSpec((ROW_TILE, d), row),
        out_shape=jax.ShapeDtypeStruct((n, d), F32),
        compiler_params=_params("parallel"),
        name="resid_mlp",
    )(h, y, wo, bo, g, w1, w2, gf)


def kernel(x, attn_w_qkv, attn_sinks, attn_w_o, conv_w_pw1, conv_b_pw1, conv_w_dw, conv_b_dw, conv_ln_g,
           conv_ln_b, conv_w_pw2, conv_b_pw2, hgrn_w_qfig, hgrn_lower_bounds, hgrn_norm_g, hgrn_w_o,
           norm_mixer, norm_mlp, mlp_w1, mlp_w2, final_norm):
    batch, seq, d = x.shape
    h = x.reshape(batch * seq, d)
    row = lambda v: v.reshape(1, -1)
    zero_bias = jnp.zeros((1, d), F32)
    w_qkv, w_pw1, w_qfig = attn_w_qkv.astype(BF16), conv_w_pw1.astype(BF16), hgrn_w_qfig.astype(BF16)
    for i in range(DEPTH):
        kind, j = i % N_MIXERS, i // N_MIXERS
        g_mix = row(norm_mixer[i])
        bias, has_bias = zero_bias, False
        mlp_weights = [(mlp_w1, i), (mlp_w2, i)]
        if kind == 0:
            qkv = _norm_proj(h, g_mix, w_qkv, j, BF16)
            y, wo, w1, w2 = _attention_core(qkv, attn_sinks[j], [(attn_w_o, j)] + mlp_weights, batch, seq)
        elif kind == 1:
            u = _norm_glu(h, g_mix, w_pw1, j, row(conv_b_pw1[j]))
            y, wo, w1, w2 = _conv_core(u, conv_w_dw[j], row(conv_b_dw[j]), row(conv_ln_g[j]), row(conv_ln_b[j]),
                                       [(conv_w_pw2, j)] + mlp_weights, batch, seq)
            bias, has_bias = row(conv_b_pw2[j]), True
        else:
            qfig = _norm_proj(h, g_mix, w_qfig, j, F32)
            y, wo, w1, w2 = _hgrn_core(qfig, hgrn_lower_bounds, row(hgrn_norm_g[j]), i,
                                       [(hgrn_w_o, j)] + mlp_weights, batch, seq)
        h = _resid_mlp(h, y, wo, bias, row(norm_mlp[i]), w1, w2, row(final_norm), has_bias, i == DEPTH - 1)
    return h.reshape(batch, seq, d)
```

```python
import functools
import math

import jax
import jax.numpy as jnp
from jax import lax
from jax.experimental import pallas as pl
from jax.experimental.pallas import tpu as pltpu

F32 = jnp.float32
BF16 = jnp.bfloat16

D_MODEL = 1024
DEPTH = 4
N_MIXERS = 3
ATTN_HEADS = 16
ATTN_KV_HEADS = 2
HEAD_DIM = 64
WINDOW = 128
QKV_DIM = (ATTN_HEADS + 2 * ATTN_KV_HEADS) * HEAD_DIM
CONV_WIDTH = 31
HGRN_HEADS = 8
HGRN_DK = 128
D_FF = 4 * D_MODEL
NORM_EPS = 1e-6
MASK_VALUE = -1e30
LOG2E = math.log2(math.e)

V7X_VMEM_BYTES = 64 * 1024 * 1024
VMEM_LIMIT_BYTES = V7X_VMEM_BYTES - 8 * 1024 * 1024
SUBLANES = 8
LANES = 128

ROW_TILE = 1024
FF_CHUNK = 1024
ATTN_TILE = 1024
CONV_TILE = 512
CONV_HALO = 32
CONV_ROWS = 32
CONV_LANES = 128
NORM_ROWS = 512
HGRN_CHUNK = 128
HGRN_STEP_CHUNKS = 4
HGRN_LEVELS = (64, 32, 16, 8, 4, 2, 1)
HEAD_SKEW = 4


def _params(*semantics):
    return pltpu.CompilerParams(dimension_semantics=semantics, vmem_limit_bytes=VMEM_LIMIT_BYTES)


def _resident(shape, lead=None):
    if lead is None:
        return pl.BlockSpec(shape, lambda *_: (0,) * len(shape), pipeline_mode=pl.Buffered(1))
    return pl.BlockSpec((None,) + tuple(shape[1:]), lambda *_: (lead,) + (0,) * (len(shape) - 1),
                        pipeline_mode=pl.Buffered(1))


def _rms_norm(x, g):
    ms = jnp.mean(x * x, axis=-1, keepdims=True)
    return x * lax.rsqrt(ms + NORM_EPS) * g


def _sigmoid(x):
    return 0.5 * jnp.tanh(0.5 * x) + 0.5


def _silu(x):
    h = 0.5 * x
    return h + h * jnp.tanh(h)


def _zero_like(x):
    bits = lax.bitcast_convert_type(x, jnp.uint32)
    sixteen = jnp.uint32(16)
    return lax.bitcast_convert_type((bits >> sixteen) >> sixteen, F32)


N_CAST = 3


def _cast_specs(weights, n_steps, step_index):
    in_specs, out_specs, out_shapes = [], [], []
    for stack, layer in weights:
        _, rows, cols = stack.shape
        part = rows // n_steps
        in_specs.append(pl.BlockSpec((None, part, cols), lambda *g, layer=layer: (layer, step_index(*g), 0)))
        out_specs.append(pl.BlockSpec((part, cols), lambda *g: (step_index(*g), 0)))
        out_shapes.append(jax.ShapeDtypeStruct((rows, cols), BF16))
    return in_specs, out_specs, out_shapes


def _cast_weights(src_refs, dst_refs):
    for src, dst in zip(src_refs, dst_refs):
        dst[...] = src[...].astype(BF16)


def _dot(a, b):
    return jnp.dot(a, b, preferred_element_type=F32)


def _dot_nt(a, b):
    return lax.dot_general(a, b, (((1,), (1,)), ((), ())), preferred_element_type=F32)


def _norm_proj_kernel(x_ref, g_ref, w_ref, o_ref):
    hn = _rms_norm(x_ref[...], g_ref[...]).astype(BF16)
    n_out = o_ref.shape[1]
    chunk = min(n_out, FF_CHUNK)
    for c in range(0, n_out, chunk):
        width = min(chunk, n_out - c)
        o_ref[:, c:c + width] = _dot(hn, w_ref[:, c:c + width]).astype(o_ref.dtype)


def _norm_glu_kernel(x_ref, g_ref, w_ref, b_ref, o_ref):
    hn = _rms_norm(x_ref[...], g_ref[...]).astype(BF16)
    a = _dot(hn, w_ref[:, :D_MODEL]) + b_ref[:, :D_MODEL]
    gate = _dot(hn, w_ref[:, D_MODEL:]) + b_ref[:, D_MODEL:]
    o_ref[...] = a * _sigmoid(gate)


def _norm_proj(h, g, w_stack, layer, out_dtype):
    n, d = h.shape
    n_out = w_stack.shape[2]
    return pl.pallas_call(
        _norm_proj_kernel,
        grid=(n // ROW_TILE,),
        in_specs=[pl.BlockSpec((ROW_TILE, d), lambda i: (i, 0)), _resident((1, d)),
                  _resident(w_stack.shape, layer)],
        out_specs=pl.BlockSpec((ROW_TILE, n_out), lambda i: (i, 0)),
        out_shape=jax.ShapeDtypeStruct((n, n_out), out_dtype),
        compiler_params=_params("parallel"),
        name="norm_proj",
    )(h, g, w_stack)


def _norm_glu(h, g, w_stack, layer, b):
    n, d = h.shape
    return pl.pallas_call(
        _norm_glu_kernel,
        grid=(n // ROW_TILE,),
        in_specs=[pl.BlockSpec((ROW_TILE, d), lambda i: (i, 0)), _resident((1, d)),
                  _resident(w_stack.shape, layer), _resident((1, 2 * d))],
        out_specs=pl.BlockSpec((ROW_TILE, d), lambda i: (i, 0)),
        out_shape=jax.ShapeDtypeStruct((n, d), F32),
        compiler_params=_params("parallel"),
        name="norm_glu",
    )(h, g, w_stack, b)


def _attn_bias():
    qi = jnp.arange(WINDOW)[:, None]
    col = jnp.arange(4 * WINDOW)[None, :]
    si = (col // (2 * WINDOW)) * WINDOW + col % WINDOW
    second = (col // WINDOW) % 2
    dist = WINDOW + qi - si
    valid = (dist >= 0) & (dist < WINDOW)
    head = 2 * jnp.arange(ATTN_HEADS // 2)[:, None, None] + second[None]
    slopes = jnp.exp2(-8.0 * (head + 1).astype(F32) / ATTN_HEADS)
    return jnp.where(valid[None], -(slopes * dist[None].astype(F32)), MASK_VALUE)


def _attn_kernel(sink_ref, q_ref, kvc_ref, kvp_ref, bias_ref, *refs):
    o_ref = refs[N_CAST]
    _cast_weights(refs[:N_CAST], refs[N_CAST + 1:])
    first_tile = pl.program_id(1) == 0
    pair = 2 * HEAD_DIM
    nblk = ATTN_TILE // WINDOW
    group = ATTN_HEADS // ATTN_KV_HEADS
    scale = HEAD_DIM ** -0.5
    low = lax.broadcasted_iota(jnp.int32, (WINDOW, pair), 1) < HEAD_DIM
    ones_bd = jnp.concatenate([jnp.where(low, 1.0, 0.0), jnp.where(low, 0.0, 1.0)], axis=0).astype(BF16)
    prev_cols = lax.broadcasted_iota(jnp.int32, (WINDOW, 4 * WINDOW), 1) < 2 * WINDOW

    def expand(x, xr, kvh):
        first, second = (x, xr) if kvh == 0 else (xr, x)
        return jnp.concatenate([jnp.where(low, first, 0.0), jnp.where(low, 0.0, second)], axis=0).astype(BF16)

    kblk, vblk = [], []
    for r in range(nblk + 1):
        kv = (kvp_ref[...] if r == 0 else kvc_ref[(r - 1) * WINDOW:r * WINDOW, :]).astype(F32)
        k2, v2 = kv[:, :pair] * scale, kv[:, pair:]
        k2r, v2r = pltpu.roll(k2, HEAD_DIM, 1), pltpu.roll(v2, HEAD_DIM, 1)
        kblk.append([expand(k2, k2r, kvh) for kvh in range(ATTN_KV_HEADS)])
        vblk.append([jnp.concatenate([expand(v2, v2r, kvh), ones_bd], axis=1) for kvh in range(ATTN_KV_HEADS)])

    kb = [[jnp.concatenate([kblk[r][kvh], kblk[r + 1][kvh]], axis=0) for kvh in range(ATTN_KV_HEADS)]
          for r in range(nblk)]
    vb = [[jnp.concatenate([vblk[r][kvh], vblk[r + 1][kvh]], axis=0) for kvh in range(ATTN_KV_HEADS)]
          for r in range(nblk)]

    def scores(r, j):
        rows, cols = slice(r * WINDOW, (r + 1) * WINDOW), slice(j * pair, (j + 1) * pair)
        s = _dot_nt(q_ref[rows, cols], kb[r][(2 * j) // group]) + bias_ref[j]
        if r == 0:
            s = jnp.where(first_tile & prev_cols, MASK_VALUE, s)
        return s

    def weigh(r, j, s):
        rows, cols = slice(r * WINDOW, (r + 1) * WINDOW), slice(j * pair, (j + 1) * pair)
        sa0, sb0, sa1, sb1 = (s[:, i * WINDOW:(i + 1) * WINDOW] for i in range(4))
        sink_a, sink_b = sink_ref[2 * j], sink_ref[2 * j + 1]
        ma = jnp.maximum(jnp.max(jnp.maximum(sa0, sa1), axis=-1, keepdims=True), sink_a)
        mb = jnp.maximum(jnp.max(jnp.maximum(sb0, sb1), axis=-1, keepdims=True), sink_b)
        e = jnp.concatenate([jnp.exp(sa0 - ma), jnp.exp(sb0 - mb), jnp.exp(sa1 - ma), jnp.exp(sb1 - mb)],
                            axis=1).astype(BF16)
        ov = _dot(e, vb[r][(2 * j) // group])
        sink_term = jnp.where(low, jnp.exp(sink_a - ma), jnp.exp(sink_b - mb))
        o_ref[rows, cols] = (ov[:, :pair] / (ov[:, pair:] + sink_term)).astype(o_ref.dtype)

    for r in range(nblk):
        for j in range(ATTN_HEADS // 2):
            weigh(r, j, scores(r, j))


def _attention_core(qkv, sinks, weights, batch, seq):
    n = batch * seq
    nt = seq // ATTN_TILE
    blocks_per_tile = ATTN_TILE // WINDOW
    q_cols = ATTN_HEADS * HEAD_DIM
    kv_cols = 2 * ATTN_KV_HEADS * HEAD_DIM
    kv_col_block = q_cols // kv_cols
    bias = _attn_bias()
    cast_in, cast_out, cast_shapes = _cast_specs(weights, batch * nt, lambda b, i: b * nt + i)

    def prev_block(b, i):
        return ((b * nt + i) * blocks_per_tile - jnp.minimum(i, 1), kv_col_block)

    return pl.pallas_call(
        _attn_kernel,
        grid=(batch, nt),
        in_specs=[
            pl.BlockSpec(memory_space=pltpu.SMEM),
            pl.BlockSpec((ATTN_TILE, q_cols), lambda b, i: (b * nt + i, 0)),
            pl.BlockSpec((ATTN_TILE, kv_cols), lambda b, i: (b * nt + i, kv_col_block)),
            pl.BlockSpec((WINDOW, kv_cols), prev_block),
            _resident(bias.shape),
        ] + cast_in,
        out_specs=[pl.BlockSpec((ATTN_TILE, q_cols), lambda b, i: (b * nt + i, 0))] + cast_out,
        out_shape=[jax.ShapeDtypeStruct((n, q_cols), BF16)] + cast_shapes,
        compiler_params=_params("parallel", "parallel"),
        name="attn_core",
    )(sinks, qkv, qkv, qkv, bias, *[stack for stack, _ in weights])


def _conv_kernel(u_ref, w_ref, bdw_ref, lng_ref, lnb_ref, *refs):
    o_ref = refs[N_CAST]
    buf_ref, wb_ref, c_ref = refs[2 * N_CAST + 1:]
    _cast_weights(refs[:N_CAST], refs[N_CAST + 1:2 * N_CAST + 1])
    t = pl.program_id(1)
    tile = u_ref.shape[0]

    @pl.when((pl.program_id(0) == 0) & (t == 0))
    def _():
        for j in range(CONV_WIDTH):
            wb_ref[j] = jnp.broadcast_to(w_ref[j:j + 1, :], (SUBLANES, D_MODEL))

    @pl.when(t == 0)
    def _():
        buf_ref[0, 0:CONV_HALO, :] = jnp.zeros((CONV_HALO, D_MODEL), F32)

    @pl.when(t > 0)
    def _():
        buf_ref[0, 0:CONV_HALO, :] = buf_ref[0, tile:tile + CONV_HALO, :]

    buf_ref[0, CONV_HALO:, :] = u_ref[...]
    shifted_rows = tile + CONV_HALO - SUBLANES
    for r in range(1, SUBLANES):
        buf_ref[r, 0:shifted_rows, :] = buf_ref[0, r:r + shifted_rows, :]

    first_tap_row = CONV_HALO - (CONV_WIDTH - 1)
    groups = CONV_ROWS // SUBLANES

    def conv_rows(i, carry):
        base = pl.multiple_of(i * CONV_ROWS, CONV_ROWS)
        acc = None
        for l in range(0, D_MODEL, CONV_LANES):
            lanes = slice(l, l + CONV_LANES)
            start = bdw_ref[:, lanes]
            if acc is not None:
                start = start + _zero_like(acc[0])
            acc = jnp.broadcast_to(start, (groups, SUBLANES, CONV_LANES))
            for r in range(SUBLANES):
                taps = [j for j in range(CONV_WIDTH) if (first_tap_row + j) % SUBLANES == r]
                a0 = (first_tap_row + taps[0]) // SUBLANES
                span = (first_tap_row + taps[-1]) // SUBLANES - a0 + groups
                big = buf_ref[r, pl.ds(base + a0 * SUBLANES, span * SUBLANES), lanes]
                big = big.reshape(span, SUBLANES, CONV_LANES)
                for j in taps:
                    a = (first_tap_row + j) // SUBLANES - a0
                    acc = acc + wb_ref[j, :, lanes] * big[a:a + groups]
            c_ref[pl.ds(base, CONV_ROWS), lanes] = acc.reshape(CONV_ROWS, CONV_LANES)
        return carry

    lax.fori_loop(0, tile // CONV_ROWS, conv_rows, 0)

    def norm_rows(i, carry):
        base = pl.multiple_of(i * NORM_ROWS, NORM_ROWS)
        c = c_ref[pl.ds(base, NORM_ROWS), :]
        mu = jnp.mean(c, axis=-1, keepdims=True)
        cen = c - mu
        var = jnp.mean(cen * cen, axis=-1, keepdims=True)
        y = cen * lax.rsqrt(var + NORM_EPS) * lng_ref[...] + lnb_ref[...]
        o_ref[pl.ds(base, NORM_ROWS), :] = _silu(y).astype(o_ref.dtype)
        return carry

    lax.fori_loop(0, tile // NORM_ROWS, norm_rows, 0)


def _conv_core(u, w_dw, b_dw, ln_g, ln_b, weights, batch, seq):
    n, d = u.shape
    nt = seq // CONV_TILE
    row = lambda b, t: (b * nt + t, 0)
    cast_in, cast_out, cast_shapes = _cast_specs(weights, batch * nt, lambda b, t: b * nt + t)
    return pl.pallas_call(
        _conv_kernel,
        grid=(batch, nt),
        in_specs=[pl.BlockSpec((CONV_TILE, d), row), _resident((CONV_WIDTH, d)),
                  _resident((1, d)), _resident((1, d)), _resident((1, d))] + cast_in,
        out_specs=[pl.BlockSpec((CONV_TILE, d), row)] + cast_out,
        out_shape=[jax.ShapeDtypeStruct((n, d), BF16)] + cast_shapes,
        scratch_shapes=[pltpu.VMEM((SUBLANES, CONV_TILE + CONV_HALO, d), F32),
                        pltpu.VMEM((CONV_WIDTH, SUBLANES, d), F32),
                        pltpu.VMEM((CONV_TILE, d), F32)],
        compiler_params=_params("arbitrary", "arbitrary"),
        name="conv_core",
    )(u, w_dw, b_dw, ln_g, ln_b, *[stack for stack, _ in weights])


def _hgrn_constants():
    c = HGRN_CHUNK
    t = jnp.arange(c)[:, None]
    s = jnp.arange(c)[None, :]
    tri = (s <= t).astype(BF16)
    masks = []
    for hs in HGRN_LEVELS[1:]:
        same = (t // (2 * hs)) == (s // (2 * hs))
        masks.append(same & ((t // hs) % 2 == 1) & ((s // hs) % 2 == 0))
    masks.append(t == s)
    return tri, jnp.stack(masks).astype(F32)


def _hgrn_kernel(h_ref, gmix_ref, wq_ref, lbp_ref, ng_ref, tri_ref, mask_ref, *refs, layer):
    o_ref = refs[N_CAST]
    q_s, k_s, b_s, v_s, g_s, st_ref = refs[2 * N_CAST + 1:]
    _cast_weights(refs[:N_CAST], refs[N_CAST + 1:2 * N_CAST + 1])
    chunk = HGRN_CHUNK
    blocks = chunk // SUBLANES

    @pl.when(pl.program_id(1) == 0)
    def _():
        st_ref[...] = jnp.zeros(st_ref.shape, F32)

    p = lbp_ref[...]
    e = jnp.exp(p - jnp.max(p, axis=0, keepdims=True))
    sm = e / jnp.sum(e, axis=0, keepdims=True)
    lb = jnp.zeros((1, D_MODEL), F32)
    for l in range(1, layer + 1):
        lb = lb + sm[l:l + 1, :]

    def prepare(c):
        rows = slice(c * chunk, (c + 1) * chunk)
        hn = _rms_norm(h_ref[rows, :], gmix_ref[...]).astype(BF16)
        q_s[c] = _silu(_dot(hn, wq_ref[:, 0:D_MODEL])) * (HGRN_DK ** -0.5)
        v_s[c] = _dot(hn, wq_ref[:, 2 * D_MODEL:3 * D_MODEL]).astype(BF16)
        g_s[c] = _silu(_dot(hn, wq_ref[:, 3 * D_MODEL:4 * D_MODEL]))
        f = lb + (1.0 - lb) * _sigmoid(_dot(hn, wq_ref[:, D_MODEL:2 * D_MODEL]))
        k_s[c] = 1.0 - f
        logf = jnp.log(f)
        g1 = logf.astype(BF16)
        r1 = logf - g1.astype(F32)
        g2 = r1.astype(BF16)
        g3 = (r1 - g2.astype(F32)).astype(BF16)
        tri = tri_ref[...]
        b_s[c] = (_dot(tri, g1) + _dot(tri, g2) + _dot(tri, g3)) * LOG2E

    row_in_blk = lax.broadcasted_iota(jnp.int32, (blocks, SUBLANES, HGRN_DK), 1)
    upper_of_8 = row_in_blk >= 4
    sign4 = jnp.where(upper_of_8, 1.0, -1.0)
    sign2 = jnp.where(row_in_blk % 4 >= 2, 1.0, -1.0)

    def scores(c, hd):
        sl = slice(hd * HGRN_DK, (hd + 1) * HGRN_DK)
        qh, kh, bh = q_s[c, :, sl], k_s[c, :, sl], b_s[c, :, sl]
        kb = kh.astype(BF16)
        state = st_ref[hd]

        o = _dot_nt((qh * jnp.exp2(bh)).astype(BF16), state.astype(BF16))

        a = jnp.zeros((chunk, chunk), F32)
        for lvl, hs in enumerate(HGRN_LEVELS[:4]):
            q_parts, k_parts = [], []
            zeros = jnp.zeros((hs, HGRN_DK), F32)
            for s0 in range(0, chunk, 2 * hs):
                pivot = bh[s0 + hs - 1:s0 + hs, :]
                lo, up = slice(s0, s0 + hs), slice(s0 + hs, s0 + 2 * hs)
                q_parts += [zeros, qh[up] * jnp.exp2(bh[up] - pivot)]
                k_parts += [kh[lo] * jnp.exp2(pivot - bh[lo]), zeros]
            al = _dot_nt(jnp.concatenate(q_parts, axis=0).astype(BF16),
                         jnp.concatenate(k_parts, axis=0).astype(BF16))
            a = a + (al if lvl == 0 else al * mask_ref[lvl - 1])

        b3 = bh.reshape(blocks, SUBLANES, HGRN_DK)
        q3 = qh.reshape(blocks, SUBLANES, HGRN_DK)
        k3 = kh.reshape(blocks, SUBLANES, HGRN_DK)
        pivot4 = b3[:, 3:4, :]
        pivot2 = jnp.where(upper_of_8, b3[:, 5:6, :], b3[:, 1:2, :])
        for lvl, w3 in ((4, jnp.exp2((b3 - pivot4) * sign4)), (5, jnp.exp2((b3 - pivot2) * sign2))):
            al = _dot_nt((q3 * w3).reshape(chunk, HGRN_DK).astype(BF16),
                         (k3 * w3).reshape(chunk, HGRN_DK).astype(BF16))
            a = a + al * mask_ref[lvl - 1]
        a = a + _dot_nt((qh * (1.0 - kh)).astype(BF16), kb) * mask_ref[5]
        a = a + _dot_nt(qh.astype(BF16), kb) * mask_ref[6]
        return o, a, state

    def mix(c, hd, o, a, state):
        sl = slice(hd * HGRN_DK, (hd + 1) * HGRN_DK)
        rows = slice(c * chunk, (c + 1) * chunk)
        kh, bh = k_s[c, :, sl], b_s[c, :, sl]
        vb = v_s[c, :, sl]
        b_last = bh[chunk - 1:chunk, :]
        o = o + _dot(a.astype(BF16), vb)
        k_dec = (kh * jnp.exp2(b_last - bh)).astype(BF16)
        st_ref[hd] = state * jnp.exp2(b_last) + _dot(vb.astype(F32).T.astype(BF16), k_dec)
        return o

    def readout(c, hd, o):
        sl = slice(hd * HGRN_DK, (hd + 1) * HGRN_DK)
        rows = slice(c * chunk, (c + 1) * chunk)
        on = o * lax.rsqrt(jnp.mean(o * o, axis=-1, keepdims=True) + NORM_EPS)
        o_ref[rows, sl] = (on * ng_ref[:, sl] * g_s[c, :, sl]).astype(o_ref.dtype)

    items = [(c, hd) for c in range(HGRN_STEP_CHUNKS) for hd in range(HGRN_HEADS)]
    prepare(0)
    stage1, stage2 = {}, {}
    for i in range(len(items) + 2 * HEAD_SKEW):
        if i < len(items):
            stage1[i] = scores(*items[i])
        if 0 <= i - HEAD_SKEW < len(items):
            stage2[i - HEAD_SKEW] = mix(*items[i - HEAD_SKEW], *stage1.pop(i - HEAD_SKEW))
        if 0 <= i - 2 * HEAD_SKEW < len(items):
            readout(*items[i - 2 * HEAD_SKEW], stage2.pop(i - 2 * HEAD_SKEW))
        if 1 <= i < HGRN_STEP_CHUNKS:
            prepare(i)


def _hgrn_core(h, g_mix, wq_stack, wq_layer, lower_bounds, norm_g, layer, weights, batch, seq):
    n, d = h.shape
    rows = HGRN_STEP_CHUNKS * HGRN_CHUNK
    nc = seq // rows
    tri, masks = _hgrn_constants()
    cast_in, cast_out, cast_shapes = _cast_specs(weights, batch * nc, lambda b, c: b * nc + c)
    chunk_scratch = lambda dtype: pltpu.VMEM((HGRN_STEP_CHUNKS, HGRN_CHUNK, d), dtype)
    return pl.pallas_call(
        functools.partial(_hgrn_kernel, layer=layer),
        grid=(batch, nc),
        in_specs=[pl.BlockSpec((rows, d), lambda b, c: (b * nc + c, 0)), _resident((1, d)),
                  _resident(wq_stack.shape, wq_layer),
                  _resident(lower_bounds.shape), _resident((1, d)),
                  _resident(tri.shape), _resident(masks.shape)] + cast_in,
        out_specs=[pl.BlockSpec((rows, d), lambda b, c: (b * nc + c, 0))] + cast_out,
        out_shape=[jax.ShapeDtypeStruct((n, d), BF16)] + cast_shapes,
        scratch_shapes=[chunk_scratch(F32), chunk_scratch(F32), chunk_scratch(F32), chunk_scratch(BF16),
                        chunk_scratch(F32), pltpu.VMEM((HGRN_HEADS, HGRN_DK, HGRN_DK), F32)],
        compiler_params=_params("arbitrary", "arbitrary"),
        name="hgrn_core",
    )(h, g_mix, wq_stack, lower_bounds, norm_g, tri, masks, *[stack for stack, _ in weights])


def _resid_mlp_kernel(h_ref, y_ref, wo_ref, bo_ref, g_ref, w1_ref, w2_ref, gf_ref, o_ref, *,
                      has_bias, final_norm):
    h2 = h_ref[...] + _dot(y_ref[...], wo_ref[...])
    if has_bias:
        h2 = h2 + bo_ref[...]
    hn = _rms_norm(h2, g_ref[...]).astype(BF16)
    acc = h2
    for c in range(0, D_FF, FF_CHUNK):
        up = jnp.maximum(_dot(hn, w1_ref[:, c:c + FF_CHUNK]), 0.0)
        acc = acc + _dot((up * up).astype(BF16), w2_ref[c:c + FF_CHUNK, :])
    if final_norm:
        acc = _rms_norm(acc, gf_ref[...])
    o_ref[...] = acc


def _resid_mlp(h, y, wo, bo, g, w1, w2, gf, has_bias, final_norm):
    n, d = h.shape
    row = lambda i: (i, 0)
    return pl.pallas_call(
        functools.partial(_resid_mlp_kernel, has_bias=has_bias, final_norm=final_norm),
        grid=(n // ROW_TILE,),
        in_specs=[pl.BlockSpec((ROW_TILE, d), row), pl.BlockSpec((ROW_TILE, d), row),
                  _resident(wo.shape), _resident((1, d)), _resident((1, d)),
                  _resident(w1.shape), _resident(w2.shape), _resident((1, d))],
        out_specs=pl.BlockSpec((ROW_TILE, d), row),
        out_shape=jax.ShapeDtypeStruct((n, d), F32),
        compiler_params=_params("parallel"),
        name="resid_mlp",
    )(h, y, wo, bo, g, w1, w2, gf)


def kernel(x, attn_w_qkv, attn_sinks, attn_w_o, conv_w_pw1, conv_b_pw1, conv_w_dw, conv_b_dw, conv_ln_g,
           conv_ln_b, conv_w_pw2, conv_b_pw2, hgrn_w_qfig, hgrn_lower_bounds, hgrn_norm_g, hgrn_w_o,
           norm_mixer, norm_mlp, mlp_w1, mlp_w2, final_norm):
    batch, seq, d = x.shape
    h = x.reshape(batch * seq, d)
    row = lambda v: v.reshape(1, -1)
    zero_bias = jnp.zeros((1, d), F32)
    w_qkv, w_pw1, w_qfig = attn_w_qkv.astype(BF16), conv_w_pw1.astype(BF16), hgrn_w_qfig.astype(BF16)
    for i in range(DEPTH):
        kind, j = i % N_MIXERS, i // N_MIXERS
        g_mix = row(norm_mixer[i])
        bias, has_bias = zero_bias, False
        mlp_weights = [(mlp_w1, i), (mlp_w2, i)]
        if kind == 0:
            qkv = _norm_proj(h, g_mix, w_qkv, j, BF16)
            y, wo, w1, w2 = _attention_core(qkv, attn_sinks[j], [(attn_w_o, j)] + mlp_weights, batch, seq)
        elif kind == 1:
            u = _norm_glu(h, g_mix, w_pw1, j, row(conv_b_pw1[j]))
            y, wo, w1, w2 = _conv_core(u, conv_w_dw[j], row(conv_b_dw[j]), row(conv_ln_g[j]), row(conv_ln_b[j]),
                                       [(conv_w_pw2, j)] + mlp_weights, batch, seq)
            bias, has_bias = row(conv_b_pw2[j]), True
        else:
            y, wo, w1, w2 = _hgrn_core(h, g_mix, w_qfig, j, hgrn_lower_bounds, row(hgrn_norm_g[j]), i,
                                       [(hgrn_w_o, j)] + mlp_weights, batch, seq)
        h = _resid_mlp(h, y, wo, bias, row(norm_mlp[i]), w1, w2, row(final_norm), has_bias, i == DEPTH - 1)
    return h.reshape(batch, seq, d)
```

```python
import functools
import math

import jax
import jax.numpy as jnp
from jax import lax
from jax.experimental import pallas as pl
from jax.experimental.pallas import tpu as pltpu

F32 = jnp.float32
BF16 = jnp.bfloat16

D_MODEL = 1024
DEPTH = 4
N_MIXERS = 3
ATTN_HEADS = 16
ATTN_KV_HEADS = 2
HEAD_DIM = 64
WINDOW = 128
QKV_DIM = (ATTN_HEADS + 2 * ATTN_KV_HEADS) * HEAD_DIM
CONV_WIDTH = 31
HGRN_HEADS = 8
HGRN_DK = 128
D_FF = 4 * D_MODEL
NORM_EPS = 1e-6
MASK_VALUE = -1e30
LOG2E = math.log2(math.e)

V7X_VMEM_BYTES = 64 * 1024 * 1024
VMEM_LIMIT_BYTES = V7X_VMEM_BYTES - 8 * 1024 * 1024
SUBLANES = 8
LANES = 128

ROW_TILE = 1024
FF_CHUNK = 1024
ATTN_TILE = 2048
CONV_TILE = 512
CONV_HALO = 32
CONV_ROWS = 32
CONV_LANES = 128
NORM_ROWS = 512
HGRN_CHUNK = 128
HGRN_STEP_CHUNKS = 4
HGRN_LEVELS = (64, 32, 16, 8, 4, 2, 1)
HEAD_SKEW = 4


def _params(*semantics):
    return pltpu.CompilerParams(dimension_semantics=semantics, vmem_limit_bytes=VMEM_LIMIT_BYTES)


def _resident(shape):
    return pl.BlockSpec(shape, lambda *_: (0,) * len(shape), pipeline_mode=pl.Buffered(1))


def _rms_norm(x, g):
    ms = jnp.mean(x * x, axis=-1, keepdims=True)
    return x * lax.rsqrt(ms + NORM_EPS) * g


def _sigmoid(x):
    return 0.5 * jnp.tanh(0.5 * x) + 0.5


def _silu(x):
    h = 0.5 * x
    return h + h * jnp.tanh(h)


def _zero_like(x):
    bits = lax.bitcast_convert_type(x, jnp.uint32)
    sixteen = jnp.uint32(16)
    return lax.bitcast_convert_type((bits >> sixteen) >> sixteen, F32)


def _cast_specs(weights, n_steps, step_index):
    in_specs, out_specs, out_shapes = [], [], []
    for stack, layer in weights:
        _, rows, cols = stack.shape
        part = rows // n_steps
        in_specs.append(pl.BlockSpec((None, part, cols), lambda *g, layer=layer: (layer, step_index(*g), 0)))
        out_specs.append(pl.BlockSpec((part, cols), lambda *g: (step_index(*g), 0)))
        out_shapes.append(jax.ShapeDtypeStruct((rows, cols), BF16))
    return in_specs, out_specs, out_shapes


def _cast_weights(src_refs, dst_refs):
    for src, dst in zip(src_refs, dst_refs):
        dst[...] = src[...].astype(BF16)


def _dot(a, b):
    return jnp.dot(a, b, preferred_element_type=F32)


def _dot_nt(a, b):
    return lax.dot_general(a, b, (((1,), (1,)), ((), ())), preferred_element_type=F32)


def _norm_proj_kernel(x_ref, g_ref, w_ref, o_ref):
    hn = _rms_norm(x_ref[...], g_ref[...]).astype(BF16)
    n_out = o_ref.shape[1]
    chunk = min(n_out, FF_CHUNK)
    for c in range(0, n_out, chunk):
        width = min(chunk, n_out - c)
        o_ref[:, c:c + width] = _dot(hn, w_ref[:, c:c + width]).astype(o_ref.dtype)


def _norm_glu_kernel(x_ref, g_ref, w_ref, b_ref, o_ref):
    hn = _rms_norm(x_ref[...], g_ref[...]).astype(BF16)
    a = _dot(hn, w_ref[:, :D_MODEL]) + b_ref[:, :D_MODEL]
    gate = _dot(hn, w_ref[:, D_MODEL:]) + b_ref[:, D_MODEL:]
    o_ref[...] = a * _sigmoid(gate)


def _norm_proj(h, g, w, out_dtype):
    n, d = h.shape
    n_out = w.shape[1]
    return pl.pallas_call(
        _norm_proj_kernel,
        grid=(n // ROW_TILE,),
        in_specs=[pl.BlockSpec((ROW_TILE, d), lambda i: (i, 0)), _resident((1, d)),
                  _resident(w.shape)],
        out_specs=pl.BlockSpec((ROW_TILE, n_out), lambda i: (i, 0)),
        out_shape=jax.ShapeDtypeStruct((n, n_out), out_dtype),
        compiler_params=_params("parallel"),
        name="norm_proj",
    )(h, g, w)


def _norm_glu(h, g, w, b):
    n, d = h.shape
    return pl.pallas_call(
        _norm_glu_kernel,
        grid=(n // ROW_TILE,),
        in_specs=[pl.BlockSpec((ROW_TILE, d), lambda i: (i, 0)), _resident((1, d)),
                  _resident(w.shape), _resident((1, 2 * d))],
        out_specs=pl.BlockSpec((ROW_TILE, d), lambda i: (i, 0)),
        out_shape=jax.ShapeDtypeStruct((n, d), F32),
        compiler_params=_params("parallel"),
        name="norm_glu",
    )(h, g, w, b)


def _attn_bias():
    qi = jnp.arange(WINDOW)[:, None]
    col = jnp.arange(4 * WINDOW)[None, :]
    si = (col // (2 * WINDOW)) * WINDOW + col % WINDOW
    second = (col // WINDOW) % 2
    dist = WINDOW + qi - si
    valid = (dist >= 0) & (dist < WINDOW)
    head = 2 * jnp.arange(ATTN_HEADS // 2)[:, None, None] + second[None]
    slopes = jnp.exp2(-8.0 * (head + 1).astype(F32) / ATTN_HEADS)
    return jnp.where(valid[None], -(slopes * dist[None].astype(F32)), MASK_VALUE)


def _attn_kernel(sink_ref, q_ref, kvc_ref, kvp_ref, bias_ref, *refs, n_cast):
    o_ref = refs[n_cast]
    _cast_weights(refs[:n_cast], refs[n_cast + 1:])
    first_tile = pl.program_id(1) == 0
    pair = 2 * HEAD_DIM
    nblk = ATTN_TILE // WINDOW
    group = ATTN_HEADS // ATTN_KV_HEADS
    scale = HEAD_DIM ** -0.5
    low = lax.broadcasted_iota(jnp.int32, (WINDOW, pair), 1) < HEAD_DIM
    ones_bd = jnp.concatenate([jnp.where(low, 1.0, 0.0), jnp.where(low, 0.0, 1.0)], axis=0).astype(BF16)
    prev_cols = lax.broadcasted_iota(jnp.int32, (WINDOW, 4 * WINDOW), 1) < 2 * WINDOW

    def expand(x, xr, kvh):
        first, second = (x, xr) if kvh == 0 else (xr, x)
        return jnp.concatenate([jnp.where(low, first, 0.0), jnp.where(low, 0.0, second)], axis=0).astype(BF16)

    kblk, vblk = [], []
    for r in range(nblk + 1):
        kv = (kvp_ref[...] if r == 0 else kvc_ref[(r - 1) * WINDOW:r * WINDOW, :]).astype(F32)
        k2, v2 = kv[:, :pair] * scale, kv[:, pair:]
        k2r, v2r = pltpu.roll(k2, HEAD_DIM, 1), pltpu.roll(v2, HEAD_DIM, 1)
        kblk.append([expand(k2, k2r, kvh) for kvh in range(ATTN_KV_HEADS)])
        vblk.append([jnp.concatenate([expand(v2, v2r, kvh), ones_bd], axis=1) for kvh in range(ATTN_KV_HEADS)])

    kb = [[jnp.concatenate([kblk[r][kvh], kblk[r + 1][kvh]], axis=0) for kvh in range(ATTN_KV_HEADS)]
          for r in range(nblk)]
    vb = [[jnp.concatenate([vblk[r][kvh], vblk[r + 1][kvh]], axis=0) for kvh in range(ATTN_KV_HEADS)]
          for r in range(nblk)]

    def scores(r, j):
        rows, cols = slice(r * WINDOW, (r + 1) * WINDOW), slice(j * pair, (j + 1) * pair)
        s = _dot_nt(q_ref[rows, cols], kb[r][(2 * j) // group]) + bias_ref[j]
        if r == 0:
            s = jnp.where(first_tile & prev_cols, MASK_VALUE, s)
        return s

    def weigh(r, j, s):
        rows, cols = slice(r * WINDOW, (r + 1) * WINDOW), slice(j * pair, (j + 1) * pair)
        sa0, sb0, sa1, sb1 = (s[:, i * WINDOW:(i + 1) * WINDOW] for i in range(4))
        sink_a, sink_b = sink_ref[2 * j], sink_ref[2 * j + 1]
        ma = jnp.maximum(jnp.max(jnp.maximum(sa0, sa1), axis=-1, keepdims=True), sink_a)
        mb = jnp.maximum(jnp.max(jnp.maximum(sb0, sb1), axis=-1, keepdims=True), sink_b)
        e = jnp.concatenate([jnp.exp(sa0 - ma), jnp.exp(sb0 - mb), jnp.exp(sa1 - ma), jnp.exp(sb1 - mb)],
                            axis=1).astype(BF16)
        ov = _dot(e, vb[r][(2 * j) // group])
        sink_term = jnp.where(low, jnp.exp(sink_a - ma), jnp.exp(sink_b - mb))
        o_ref[rows, cols] = (ov[:, :pair] / (ov[:, pair:] + sink_term)).astype(o_ref.dtype)

    for r in range(nblk):
        for j in range(ATTN_HEADS // 2):
            weigh(r, j, scores(r, j))


def _attention_core(qkv, sinks, weights, batch, seq):
    n = batch * seq
    nt = seq // ATTN_TILE
    blocks_per_tile = ATTN_TILE // WINDOW
    q_cols = ATTN_HEADS * HEAD_DIM
    kv_cols = 2 * ATTN_KV_HEADS * HEAD_DIM
    kv_col_block = q_cols // kv_cols
    bias = _attn_bias()
    cast_in, cast_out, cast_shapes = _cast_specs(weights, batch * nt, lambda b, i: b * nt + i)

    def prev_block(b, i):
        return ((b * nt + i) * blocks_per_tile - jnp.minimum(i, 1), kv_col_block)

    return pl.pallas_call(
        functools.partial(_attn_kernel, n_cast=len(weights)),
        grid=(batch, nt),
        in_specs=[
            pl.BlockSpec(memory_space=pltpu.SMEM),
            pl.BlockSpec((ATTN_TILE, q_cols), lambda b, i: (b * nt + i, 0)),
            pl.BlockSpec((ATTN_TILE, kv_cols), lambda b, i: (b * nt + i, kv_col_block)),
            pl.BlockSpec((WINDOW, kv_cols), prev_block),
            _resident(bias.shape),
        ] + cast_in,
        out_specs=[pl.BlockSpec((ATTN_TILE, q_cols), lambda b, i: (b * nt + i, 0))] + cast_out,
        out_shape=[jax.ShapeDtypeStruct((n, q_cols), BF16)] + cast_shapes,
        compiler_params=_params("parallel", "parallel"),
        name="attn_core",
    )(sinks, qkv, qkv, qkv, bias, *[stack for stack, _ in weights])


def _conv_kernel(u_ref, w_ref, bdw_ref, lng_ref, lnb_ref, *refs, n_cast):
    o_ref = refs[n_cast]
    buf_ref, wb_ref, c_ref = refs[2 * n_cast + 1:]
    _cast_weights(refs[:n_cast], refs[n_cast + 1:2 * n_cast + 1])
    t = pl.program_id(1)
    tile = u_ref.shape[0]

    @pl.when((pl.program_id(0) == 0) & (t == 0))
    def _():
        for j in range(CONV_WIDTH):
            wb_ref[j] = jnp.broadcast_to(w_ref[j:j + 1, :], (SUBLANES, D_MODEL))

    @pl.when(t == 0)
    def _():
        buf_ref[0, 0:CONV_HALO, :] = jnp.zeros((CONV_HALO, D_MODEL), F32)

    @pl.when(t > 0)
    def _():
        buf_ref[0, 0:CONV_HALO, :] = buf_ref[0, tile:tile + CONV_HALO, :]

    buf_ref[0, CONV_HALO:, :] = u_ref[...]
    shifted_rows = tile + CONV_HALO - SUBLANES
    for r in range(1, SUBLANES):
        buf_ref[r, 0:shifted_rows, :] = buf_ref[0, r:r + shifted_rows, :]

    first_tap_row = CONV_HALO - (CONV_WIDTH - 1)
    groups = CONV_ROWS // SUBLANES

    def conv_rows(i, carry):
        base = pl.multiple_of(i * CONV_ROWS, CONV_ROWS)
        acc = None
        for l in range(0, D_MODEL, CONV_LANES):
            lanes = slice(l, l + CONV_LANES)
            start = bdw_ref[:, lanes]
            if acc is not None:
                start = start + _zero_like(acc[0])
            acc = jnp.broadcast_to(start, (groups, SUBLANES, CONV_LANES))
            for r in range(SUBLANES):
                taps = [j for j in range(CONV_WIDTH) if (first_tap_row + j) % SUBLANES == r]
                a0 = (first_tap_row + taps[0]) // SUBLANES
                span = (first_tap_row + taps[-1]) // SUBLANES - a0 + groups
                big = buf_ref[r, pl.ds(base + a0 * SUBLANES, span * SUBLANES), lanes]
                big = big.reshape(span, SUBLANES, CONV_LANES)
                for j in taps:
                    a = (first_tap_row + j) // SUBLANES - a0
                    acc = acc + wb_ref[j, :, lanes] * big[a:a + groups]
            c_ref[pl.ds(base, CONV_ROWS), lanes] = acc.reshape(CONV_ROWS, CONV_LANES)
        return carry

    lax.fori_loop(0, tile // CONV_ROWS, conv_rows, 0)

    def norm_rows(i, carry):
        base = pl.multiple_of(i * NORM_ROWS, NORM_ROWS)
        c = c_ref[pl.ds(base, NORM_ROWS), :]
        mu = jnp.mean(c, axis=-1, keepdims=True)
        cen = c - mu
        var = jnp.mean(cen * cen, axis=-1, keepdims=True)
        y = cen * lax.rsqrt(var + NORM_EPS) * lng_ref[...] + lnb_ref[...]
        o_ref[pl.ds(base, NORM_ROWS), :] = _silu(y).astype(o_ref.dtype)
        return carry

    lax.fori_loop(0, tile // NORM_ROWS, norm_rows, 0)


def _conv_core(u, w_dw, b_dw, ln_g, ln_b, weights, batch, seq):
    n, d = u.shape
    nt = seq // CONV_TILE
    row = lambda b, t: (b * nt + t, 0)
    cast_in, cast_out, cast_shapes = _cast_specs(weights, batch * nt, lambda b, t: b * nt + t)
    return pl.pallas_call(
        functools.partial(_conv_kernel, n_cast=len(weights)),
        grid=(batch, nt),
        in_specs=[pl.BlockSpec((CONV_TILE, d), row), _resident((CONV_WIDTH, d)),
                  _resident((1, d)), _resident((1, d)), _resident((1, d))] + cast_in,
        out_specs=[pl.BlockSpec((CONV_TILE, d), row)] + cast_out,
        out_shape=[jax.ShapeDtypeStruct((n, d), BF16)] + cast_shapes,
        scratch_shapes=[pltpu.VMEM((SUBLANES, CONV_TILE + CONV_HALO, d), F32),
                        pltpu.VMEM((CONV_WIDTH, SUBLANES, d), F32),
                        pltpu.VMEM((CONV_TILE, d), F32)],
        compiler_params=_params("arbitrary", "arbitrary"),
        name="conv_core",
    )(u, w_dw, b_dw, ln_g, ln_b, *[stack for stack, _ in weights])


def _hgrn_constants():
    c = HGRN_CHUNK
    t = jnp.arange(c)[:, None]
    s = jnp.arange(c)[None, :]
    tri = (s <= t).astype(BF16)
    masks = []
    for hs in HGRN_LEVELS[1:]:
        same = (t // (2 * hs)) == (s // (2 * hs))
        masks.append(same & ((t // hs) % 2 == 1) & ((s // hs) % 2 == 0))
    masks.append(t == s)
    return tri, jnp.stack(masks).astype(F32)


def _hgrn_kernel(h_ref, gmix_ref, wq_ref, lbp_ref, ng_ref, tri_ref, mask_ref, *refs, layer, n_cast):
    o_ref = refs[n_cast]
    q_s, k_s, b_s, v_s, g_s, st_ref = refs[2 * n_cast + 1:]
    _cast_weights(refs[:n_cast], refs[n_cast + 1:2 * n_cast + 1])
    chunk = HGRN_CHUNK
    blocks = chunk // SUBLANES

    @pl.when(pl.program_id(1) == 0)
    def _():
        st_ref[...] = jnp.zeros(st_ref.shape, F32)

    p = lbp_ref[...]
    e = jnp.exp(p - jnp.max(p, axis=0, keepdims=True))
    sm = e / jnp.sum(e, axis=0, keepdims=True)
    lb = jnp.zeros((1, D_MODEL), F32)
    for l in range(1, layer + 1):
        lb = lb + sm[l:l + 1, :]

    def prepare(c):
        rows = slice(c * chunk, (c + 1) * chunk)
        hn = _rms_norm(h_ref[rows, :], gmix_ref[...]).astype(BF16)
        q_s[c] = _silu(_dot(hn, wq_ref[:, 0:D_MODEL])) * (HGRN_DK ** -0.5)
        v_s[c] = _dot(hn, wq_ref[:, 2 * D_MODEL:3 * D_MODEL]).astype(BF16)
        g_s[c] = _silu(_dot(hn, wq_ref[:, 3 * D_MODEL:4 * D_MODEL]))
        f = lb + (1.0 - lb) * _sigmoid(_dot(hn, wq_ref[:, D_MODEL:2 * D_MODEL]))
        k_s[c] = 1.0 - f
        logf = jnp.log(f)
        g1 = logf.astype(BF16)
        r1 = logf - g1.astype(F32)
        g2 = r1.astype(BF16)
        g3 = (r1 - g2.astype(F32)).astype(BF16)
        tri = tri_ref[...]
        b_s[c] = (_dot(tri, g1) + _dot(tri, g2) + _dot(tri, g3)) * LOG2E

    row_in_blk = lax.broadcasted_iota(jnp.int32, (blocks, SUBLANES, HGRN_DK), 1)
    upper_of_8 = row_in_blk >= 4
    sign4 = jnp.where(upper_of_8, 1.0, -1.0)
    sign2 = jnp.where(row_in_blk % 4 >= 2, 1.0, -1.0)

    def scores(c, hd):
        sl = slice(hd * HGRN_DK, (hd + 1) * HGRN_DK)
        qh, kh, bh = q_s[c, :, sl], k_s[c, :, sl], b_s[c, :, sl]
        kb = kh.astype(BF16)
        state = st_ref[hd]

        o = _dot_nt((qh * jnp.exp2(bh)).astype(BF16), state.astype(BF16))

        a = jnp.zeros((chunk, chunk), F32)
        for lvl, hs in enumerate(HGRN_LEVELS[:4]):
            q_parts, k_parts = [], []
            zeros = jnp.zeros((hs, HGRN_DK), F32)
            for s0 in range(0, chunk, 2 * hs):
                pivot = bh[s0 + hs - 1:s0 + hs, :]
                lo, up = slice(s0, s0 + hs), slice(s0 + hs, s0 + 2 * hs)
                q_parts += [zeros, qh[up] * jnp.exp2(bh[up] - pivot)]
                k_parts += [kh[lo] * jnp.exp2(pivot - bh[lo]), zeros]
            al = _dot_nt(jnp.concatenate(q_parts, axis=0).astype(BF16),
                         jnp.concatenate(k_parts, axis=0).astype(BF16))
            a = a + (al if lvl == 0 else al * mask_ref[lvl - 1])

        b3 = bh.reshape(blocks, SUBLANES, HGRN_DK)
        q3 = qh.reshape(blocks, SUBLANES, HGRN_DK)
        k3 = kh.reshape(blocks, SUBLANES, HGRN_DK)
        pivot4 = b3[:, 3:4, :]
        pivot2 = jnp.where(upper_of_8, b3[:, 5:6, :], b3[:, 1:2, :])
        for lvl, w3 in ((4, jnp.exp2((b3 - pivot4) * sign4)), (5, jnp.exp2((b3 - pivot2) * sign2))):
            al = _dot_nt((q3 * w3).reshape(chunk, HGRN_DK).astype(BF16),
                         (k3 * w3).reshape(chunk, HGRN_DK).astype(BF16))
            a = a + al * mask_ref[lvl - 1]
        a = a + _dot_nt((qh * (1.0 - kh)).astype(BF16), kb) * mask_ref[5]
        a = a + _dot_nt(qh.astype(BF16), kb) * mask_ref[6]
        return o, a, state

    def mix(c, hd, o, a, state):
        sl = slice(hd * HGRN_DK, (hd + 1) * HGRN_DK)
        rows = slice(c * chunk, (c + 1) * chunk)
        kh, bh = k_s[c, :, sl], b_s[c, :, sl]
        vb = v_s[c, :, sl]
        b_last = bh[chunk - 1:chunk, :]
        o = o + _dot(a.astype(BF16), vb)
        k_dec = (kh * jnp.exp2(b_last - bh)).astype(BF16)
        st_ref[hd] = state * jnp.exp2(b_last) + _dot(vb.astype(F32).T.astype(BF16), k_dec)
        return o

    def readout(c, hd, o):
        sl = slice(hd * HGRN_DK, (hd + 1) * HGRN_DK)
        rows = slice(c * chunk, (c + 1) * chunk)
        on = o * lax.rsqrt(jnp.mean(o * o, axis=-1, keepdims=True) + NORM_EPS)
        o_ref[rows, sl] = (on * ng_ref[:, sl] * g_s[c, :, sl]).astype(o_ref.dtype)

    items = [(c, hd) for c in range(HGRN_STEP_CHUNKS) for hd in range(HGRN_HEADS)]
    prepare(0)
    stage1, stage2 = {}, {}
    for i in range(len(items) + 2 * HEAD_SKEW):
        if i < len(items):
            stage1[i] = scores(*items[i])
        if 0 <= i - HEAD_SKEW < len(items):
            stage2[i - HEAD_SKEW] = mix(*items[i - HEAD_SKEW], *stage1.pop(i - HEAD_SKEW))
        if 0 <= i - 2 * HEAD_SKEW < len(items):
            readout(*items[i - 2 * HEAD_SKEW], stage2.pop(i - 2 * HEAD_SKEW))
        if 1 <= i < HGRN_STEP_CHUNKS:
            prepare(i)


def _hgrn_core(h, g_mix, wq, lower_bounds, norm_g, layer, weights, batch, seq):
    n, d = h.shape
    rows = HGRN_STEP_CHUNKS * HGRN_CHUNK
    nc = seq // rows
    tri, masks = _hgrn_constants()
    cast_in, cast_out, cast_shapes = _cast_specs(weights, batch * nc, lambda b, c: b * nc + c)
    chunk_scratch = lambda dtype: pltpu.VMEM((HGRN_STEP_CHUNKS, HGRN_CHUNK, d), dtype)
    return pl.pallas_call(
        functools.partial(_hgrn_kernel, layer=layer, n_cast=len(weights)),
        grid=(batch, nc),
        in_specs=[pl.BlockSpec((rows, d), lambda b, c: (b * nc + c, 0)), _resident((1, d)),
                  _resident(wq.shape),
                  _resident(lower_bounds.shape), _resident((1, d)),
                  _resident(tri.shape), _resident(masks.shape)] + cast_in,
        out_specs=[pl.BlockSpec((rows, d), lambda b, c: (b * nc + c, 0))] + cast_out,
        out_shape=[jax.ShapeDtypeStruct((n, d), BF16)] + cast_shapes,
        scratch_shapes=[chunk_scratch(F32), chunk_scratch(F32), chunk_scratch(F32), chunk_scratch(BF16),
                        chunk_scratch(F32), pltpu.VMEM((HGRN_HEADS, HGRN_DK, HGRN_DK), F32)],
        compiler_params=_params("arbitrary", "arbitrary"),
        name="hgrn_core",
    )(h, g_mix, wq, lower_bounds, norm_g, tri, masks, *[stack for stack, _ in weights])


def _resid_mlp_kernel(h_ref, y_ref, wo_ref, bo_ref, g_ref, w1_ref, w2_ref, gf_ref, o_ref, *,
                      has_bias, final_norm):
    h2 = h_ref[...] + _dot(y_ref[...], wo_ref[...])
    if has_bias:
        h2 = h2 + bo_ref[...]
    hn = _rms_norm(h2, g_ref[...]).astype(BF16)
    acc = h2
    for c in range(0, D_FF, FF_CHUNK):
        up = jnp.maximum(_dot(hn, w1_ref[:, c:c + FF_CHUNK]), 0.0)
        acc = acc + _dot((up * up).astype(BF16), w2_ref[c:c + FF_CHUNK, :])
    if final_norm:
        acc = _rms_norm(acc, gf_ref[...])
    o_ref[...] = acc


def _resid_mlp(h, y, wo, bo, g, w1, w2, gf, has_bias, final_norm):
    n, d = h.shape
    row = lambda i: (i, 0)
    return pl.pallas_call(
        functools.partial(_resid_mlp_kernel, has_bias=has_bias, final_norm=final_norm),
        grid=(n // ROW_TILE,),
        in_specs=[pl.BlockSpec((ROW_TILE, d), row), pl.BlockSpec((ROW_TILE, d), row),
                  _resident(wo.shape), _resident((1, d)), _resident((1, d)),
                  _resident(w1.shape), _resident(w2.shape), _resident((1, d))],
        out_specs=pl.BlockSpec((ROW_TILE, d), row),
        out_shape=jax.ShapeDtypeStruct((n, d), F32),
        compiler_params=_params("parallel"),
        name="resid_mlp",
    )(h, y, wo, bo, g, w1, w2, gf)


def kernel(x, attn_w_qkv, attn_sinks, attn_w_o, conv_w_pw1, conv_b_pw1, conv_w_dw, conv_b_dw, conv_ln_g,
           conv_ln_b, conv_w_pw2, conv_b_pw2, hgrn_w_qfig, hgrn_lower_bounds, hgrn_norm_g, hgrn_w_o,
           norm_mixer, norm_mlp, mlp_w1, mlp_w2, final_norm):
    batch, seq, d = x.shape
    h = x.reshape(batch * seq, d)
    row = lambda v: v.reshape(1, -1)
    zero_bias = jnp.zeros((1, d), F32)
    stacks = {0: attn_w_qkv, 1: conv_w_pw1, 2: hgrn_w_qfig}
    w_in = attn_w_qkv[0].astype(BF16)
    for i in range(DEPTH):
        kind, j = i % N_MIXERS, i // N_MIXERS
        g_mix = row(norm_mixer[i])
        bias, has_bias = zero_bias, False
        weights = [(mlp_w1, i), (mlp_w2, i)]
        if i + 1 < DEPTH:
            weights.append((stacks[(i + 1) % N_MIXERS], (i + 1) // N_MIXERS))
        if kind == 0:
            qkv = _norm_proj(h, g_mix, w_in, BF16)
            y, wo, w1, w2, *nxt = _attention_core(qkv, attn_sinks[j], [(attn_w_o, j)] + weights, batch, seq)
        elif kind == 1:
            u = _norm_glu(h, g_mix, w_in, row(conv_b_pw1[j]))
            y, wo, w1, w2, *nxt = _conv_core(u, conv_w_dw[j], row(conv_b_dw[j]), row(conv_ln_g[j]),
                                             row(conv_ln_b[j]), [(conv_w_pw2, j)] + weights, batch, seq)
            bias, has_bias = row(conv_b_pw2[j]), True
        else:
            y, wo, w1, w2, *nxt = _hgrn_core(h, g_mix, w_in, hgrn_lower_bounds, row(hgrn_norm_g[j]), i,
                                             [(hgrn_w_o, j)] + weights, batch, seq)
        h = _resid_mlp(h, y, wo, bias, row(norm_mlp[i]), w1, w2, row(final_norm), has_bias, i == DEPTH - 1)
        w_in = nxt[0] if nxt else None
    return h.reshape(batch, seq, d)
```

```python
import functools
import math

import jax
import jax.numpy as jnp
from jax import lax
from jax.experimental import pallas as pl
from jax.experimental.pallas import tpu as pltpu

F32 = jnp.float32
BF16 = jnp.bfloat16

D_MODEL = 1024
DEPTH = 4
N_MIXERS = 3
ATTN_HEADS = 16
ATTN_KV_HEADS = 2
HEAD_DIM = 64
WINDOW = 128
QKV_DIM = (ATTN_HEADS + 2 * ATTN_KV_HEADS) * HEAD_DIM
CONV_WIDTH = 31
HGRN_HEADS = 8
HGRN_DK = 128
D_FF = 4 * D_MODEL
NORM_EPS = 1e-6
MASK_VALUE = -1e30
LOG2E = math.log2(math.e)

V7X_VMEM_BYTES = 64 * 1024 * 1024
VMEM_LIMIT_BYTES = V7X_VMEM_BYTES - 8 * 1024 * 1024
SUBLANES = 8
LANES = 128

ROW_TILE = 1024
FF_CHUNK = 1024
ATTN_TILE = 1024
CONV_TILE = 512
CONV_HALO = 32
CONV_ROWS = 32
CONV_LANES = 128
NORM_ROWS = 512
HGRN_CHUNK = 128
HGRN_STEP_CHUNKS = 4
PREP_CHUNKS = 4
HGRN_LEVELS = (64, 32, 16, 8, 4, 2, 1)
HEAD_SKEW = 4


def _params(*semantics):
    return pltpu.CompilerParams(dimension_semantics=semantics, vmem_limit_bytes=VMEM_LIMIT_BYTES)


def _resident(shape):
    return pl.BlockSpec(shape, lambda *_: (0,) * len(shape), pipeline_mode=pl.Buffered(1))


def _rms_norm(x, g):
    ms = jnp.mean(x * x, axis=-1, keepdims=True)
    return x * lax.rsqrt(ms + NORM_EPS) * g


def _sigmoid(x):
    return 0.5 * jnp.tanh(0.5 * x) + 0.5


def _silu(x):
    h = 0.5 * x
    return h + h * jnp.tanh(h)


def _zero_like(x):
    bits = lax.bitcast_convert_type(x, jnp.uint32)
    sixteen = jnp.uint32(16)
    return lax.bitcast_convert_type((bits >> sixteen) >> sixteen, F32)


def _cast_specs(weights, n_steps, step_index):
    in_specs, out_specs, out_shapes = [], [], []
    for stack, layer in weights:
        _, rows, cols = stack.shape
        part = rows // n_steps
        in_specs.append(pl.BlockSpec((None, part, cols), lambda *g, layer=layer: (layer, step_index(*g), 0)))
        out_specs.append(pl.BlockSpec((part, cols), lambda *g: (step_index(*g), 0)))
        out_shapes.append(jax.ShapeDtypeStruct((rows, cols), BF16))
    return in_specs, out_specs, out_shapes


def _cast_weights(src_refs, dst_refs):
    for src, dst in zip(src_refs, dst_refs):
        dst[...] = src[...].astype(BF16)


def _dot(a, b):
    return jnp.dot(a, b, preferred_element_type=F32)


def _dot_nt(a, b):
    return lax.dot_general(a, b, (((1,), (1,)), ((), ())), preferred_element_type=F32)


def _norm_proj_kernel(x_ref, g_ref, w_ref, o_ref):
    hn = _rms_norm(x_ref[...], g_ref[...]).astype(BF16)
    n_out = o_ref.shape[1]
    chunk = min(n_out, FF_CHUNK)
    for c in range(0, n_out, chunk):
        width = min(chunk, n_out - c)
        o_ref[:, c:c + width] = _dot(hn, w_ref[:, c:c + width]).astype(o_ref.dtype)


def _norm_glu_kernel(x_ref, g_ref, w_ref, b_ref, o_ref):
    hn = _rms_norm(x_ref[...], g_ref[...]).astype(BF16)
    a = _dot(hn, w_ref[:, :D_MODEL]) + b_ref[:, :D_MODEL]
    gate = _dot(hn, w_ref[:, D_MODEL:]) + b_ref[:, D_MODEL:]
    o_ref[...] = a * _sigmoid(gate)


def _norm_glu(h, g, w, b):
    n, d = h.shape
    return pl.pallas_call(
        _norm_glu_kernel,
        grid=(n // ROW_TILE,),
        in_specs=[pl.BlockSpec((ROW_TILE, d), lambda i: (i, 0)), _resident((1, d)),
                  _resident(w.shape), _resident((1, 2 * d))],
        out_specs=pl.BlockSpec((ROW_TILE, d), lambda i: (i, 0)),
        out_shape=jax.ShapeDtypeStruct((n, d), F32),
        compiler_params=_params("parallel"),
        name="norm_glu",
    )(h, g, w, b)


def _norm_proj(h, g, w, out_dtype):
    n, d = h.shape
    n_out = w.shape[1]
    return pl.pallas_call(
        _norm_proj_kernel,
        grid=(n // ROW_TILE,),
        in_specs=[pl.BlockSpec((ROW_TILE, d), lambda i: (i, 0)), _resident((1, d)),
                  _resident(w.shape)],
        out_specs=pl.BlockSpec((ROW_TILE, n_out), lambda i: (i, 0)),
        out_shape=jax.ShapeDtypeStruct((n, n_out), out_dtype),
        compiler_params=_params("parallel"),
        name="norm_proj",
    )(h, g, w)


def _attn_bias():
    qi = jnp.arange(WINDOW)[:, None]
    col = jnp.arange(4 * WINDOW)[None, :]
    si = (col // (2 * WINDOW)) * WINDOW + col % WINDOW
    second = (col // WINDOW) % 2
    dist = WINDOW + qi - si
    valid = (dist >= 0) & (dist < WINDOW)
    head = 2 * jnp.arange(ATTN_HEADS // 2)[:, None, None] + second[None]
    slopes = jnp.exp2(-8.0 * (head + 1).astype(F32) / ATTN_HEADS)
    return jnp.where(valid[None], -(slopes * dist[None].astype(F32)), MASK_VALUE)


def _attn_kernel(sink_ref, q_ref, kvc_ref, kvp_ref, bias_ref, *refs, n_cast):
    o_ref = refs[n_cast]
    _cast_weights(refs[:n_cast], refs[n_cast + 1:])
    first_tile = pl.program_id(1) == 0
    pair = 2 * HEAD_DIM
    nblk = ATTN_TILE // WINDOW
    group = ATTN_HEADS // ATTN_KV_HEADS
    scale = HEAD_DIM ** -0.5
    low = lax.broadcasted_iota(jnp.int32, (WINDOW, pair), 1) < HEAD_DIM
    ones_bd = jnp.concatenate([jnp.where(low, 1.0, 0.0), jnp.where(low, 0.0, 1.0)], axis=0).astype(BF16)
    prev_cols = lax.broadcasted_iota(jnp.int32, (WINDOW, 4 * WINDOW), 1) < 2 * WINDOW

    def expand(x, xr, kvh):
        first, second = (x, xr) if kvh == 0 else (xr, x)
        return jnp.concatenate([jnp.where(low, first, 0.0), jnp.where(low, 0.0, second)], axis=0).astype(BF16)

    kblk, vblk = [], []
    for r in range(nblk + 1):
        kv = (kvp_ref[...] if r == 0 else kvc_ref[(r - 1) * WINDOW:r * WINDOW, :]).astype(F32)
        k2, v2 = kv[:, :pair] * scale, kv[:, pair:]
        k2r, v2r = pltpu.roll(k2, HEAD_DIM, 1), pltpu.roll(v2, HEAD_DIM, 1)
        kblk.append([expand(k2, k2r, kvh) for kvh in range(ATTN_KV_HEADS)])
        vblk.append([jnp.concatenate([expand(v2, v2r, kvh), ones_bd], axis=1) for kvh in range(ATTN_KV_HEADS)])

    kb = [[jnp.concatenate([kblk[r][kvh], kblk[r + 1][kvh]], axis=0) for kvh in range(ATTN_KV_HEADS)]
          for r in range(nblk)]
    vb = [[jnp.concatenate([vblk[r][kvh], vblk[r + 1][kvh]], axis=0) for kvh in range(ATTN_KV_HEADS)]
          for r in range(nblk)]

    def scores(r, j):
        rows, cols = slice(r * WINDOW, (r + 1) * WINDOW), slice(j * pair, (j + 1) * pair)
        s = _dot_nt(q_ref[rows, cols], kb[r][(2 * j) // group]) + bias_ref[j]
        if r == 0:
            s = jnp.where(first_tile & prev_cols, MASK_VALUE, s)
        return s

    def weigh(r, j, s):
        rows, cols = slice(r * WINDOW, (r + 1) * WINDOW), slice(j * pair, (j + 1) * pair)
        sa0, sb0, sa1, sb1 = (s[:, i * WINDOW:(i + 1) * WINDOW] for i in range(4))
        sink_a, sink_b = sink_ref[2 * j], sink_ref[2 * j + 1]
        ma = jnp.maximum(jnp.max(jnp.maximum(sa0, sa1), axis=-1, keepdims=True), sink_a)
        mb = jnp.maximum(jnp.max(jnp.maximum(sb0, sb1), axis=-1, keepdims=True), sink_b)
        e = jnp.concatenate([jnp.exp(sa0 - ma), jnp.exp(sb0 - mb), jnp.exp(sa1 - ma), jnp.exp(sb1 - mb)],
                            axis=1).astype(BF16)
        ov = _dot(e, vb[r][(2 * j) // group])
        sink_term = jnp.where(low, jnp.exp(sink_a - ma), jnp.exp(sink_b - mb))
        o_ref[rows, cols] = (ov[:, :pair] / (ov[:, pair:] + sink_term)).astype(o_ref.dtype)

    for r in range(nblk):
        for j in range(ATTN_HEADS // 2):
            weigh(r, j, scores(r, j))


def _attention_core(qkv, sinks, weights, batch, seq):
    n = batch * seq
    nt = seq // ATTN_TILE
    blocks_per_tile = ATTN_TILE // WINDOW
    q_cols = ATTN_HEADS * HEAD_DIM
    kv_cols = 2 * ATTN_KV_HEADS * HEAD_DIM
    kv_col_block = q_cols // kv_cols
    bias = _attn_bias()
    cast_in, cast_out, cast_shapes = _cast_specs(weights, batch * nt, lambda b, i: b * nt + i)

    def prev_block(b, i):
        return ((b * nt + i) * blocks_per_tile - jnp.minimum(i, 1), kv_col_block)

    return pl.pallas_call(
        functools.partial(_attn_kernel, n_cast=len(weights)),
        grid=(batch, nt),
        in_specs=[
            pl.BlockSpec(memory_space=pltpu.SMEM),
            pl.BlockSpec((ATTN_TILE, q_cols), lambda b, i: (b * nt + i, 0)),
            pl.BlockSpec((ATTN_TILE, kv_cols), lambda b, i: (b * nt + i, kv_col_block)),
            pl.BlockSpec((WINDOW, kv_cols), prev_block),
            _resident(bias.shape),
        ] + cast_in,
        out_specs=[pl.BlockSpec((ATTN_TILE, q_cols), lambda b, i: (b * nt + i, 0))] + cast_out,
        out_shape=[jax.ShapeDtypeStruct((n, q_cols), BF16)] + cast_shapes,
        compiler_params=_params("parallel", "parallel"),
        name="attn_core",
    )(sinks, qkv, qkv, qkv, bias, *[stack for stack, _ in weights])


def _conv_kernel(u_ref, w_ref, bdw_ref, lng_ref, lnb_ref, *refs, n_cast):
    o_ref = refs[n_cast]
    buf_ref, wb_ref, c_ref = refs[2 * n_cast + 1:]
    _cast_weights(refs[:n_cast], refs[n_cast + 1:2 * n_cast + 1])
    t = pl.program_id(1)
    tile = u_ref.shape[0]

    @pl.when((pl.program_id(0) == 0) & (t == 0))
    def _():
        for j in range(CONV_WIDTH):
            wb_ref[j] = jnp.broadcast_to(w_ref[j:j + 1, :], (SUBLANES, D_MODEL))

    @pl.when(t == 0)
    def _():
        buf_ref[0, 0:CONV_HALO, :] = jnp.zeros((CONV_HALO, D_MODEL), F32)

    @pl.when(t > 0)
    def _():
        buf_ref[0, 0:CONV_HALO, :] = buf_ref[0, tile:tile + CONV_HALO, :]

    buf_ref[0, CONV_HALO:, :] = u_ref[...]
    shifted_rows = tile + CONV_HALO - SUBLANES
    for r in range(1, SUBLANES):
        buf_ref[r, 0:shifted_rows, :] = buf_ref[0, r:r + shifted_rows, :]

    first_tap_row = CONV_HALO - (CONV_WIDTH - 1)
    groups = CONV_ROWS // SUBLANES

    def conv_rows(i, carry):
        base = pl.multiple_of(i * CONV_ROWS, CONV_ROWS)
        acc = None
        for l in range(0, D_MODEL, CONV_LANES):
            lanes = slice(l, l + CONV_LANES)
            start = bdw_ref[:, lanes]
            if acc is not None:
                start = start + _zero_like(acc[0])
            acc = jnp.broadcast_to(start, (groups, SUBLANES, CONV_LANES))
            for r in range(SUBLANES):
                taps = [j for j in range(CONV_WIDTH) if (first_tap_row + j) % SUBLANES == r]
                a0 = (first_tap_row + taps[0]) // SUBLANES
                span = (first_tap_row + taps[-1]) // SUBLANES - a0 + groups
                big = buf_ref[r, pl.ds(base + a0 * SUBLANES, span * SUBLANES), lanes]
                big = big.reshape(span, SUBLANES, CONV_LANES)
                for j in taps:
                    a = (first_tap_row + j) // SUBLANES - a0
                    acc = acc + wb_ref[j, :, lanes] * big[a:a + groups]
            c_ref[pl.ds(base, CONV_ROWS), lanes] = acc.reshape(CONV_ROWS, CONV_LANES)
        return carry

    lax.fori_loop(0, tile // CONV_ROWS, conv_rows, 0)

    def norm_rows(i, carry):
        base = pl.multiple_of(i * NORM_ROWS, NORM_ROWS)
        c = c_ref[pl.ds(base, NORM_ROWS), :]
        mu = jnp.mean(c, axis=-1, keepdims=True)
        cen = c - mu
        var = jnp.mean(cen * cen, axis=-1, keepdims=True)
        y = cen * lax.rsqrt(var + NORM_EPS) * lng_ref[...] + lnb_ref[...]
        o_ref[pl.ds(base, NORM_ROWS), :] = _silu(y).astype(o_ref.dtype)
        return carry

    lax.fori_loop(0, tile // NORM_ROWS, norm_rows, 0)


def _conv_core(u, w_dw, b_dw, ln_g, ln_b, weights, batch, seq):
    n, d = u.shape
    nt = seq // CONV_TILE
    row = lambda b, t: (b * nt + t, 0)
    cast_in, cast_out, cast_shapes = _cast_specs(weights, batch * nt, lambda b, t: b * nt + t)
    return pl.pallas_call(
        functools.partial(_conv_kernel, n_cast=len(weights)),
        grid=(batch, nt),
        in_specs=[pl.BlockSpec((CONV_TILE, d), row), _resident((CONV_WIDTH, d)),
                  _resident((1, d)), _resident((1, d)), _resident((1, d))] + cast_in,
        out_specs=[pl.BlockSpec((CONV_TILE, d), row)] + cast_out,
        out_shape=[jax.ShapeDtypeStruct((n, d), BF16)] + cast_shapes,
        scratch_shapes=[pltpu.VMEM((SUBLANES, CONV_TILE + CONV_HALO, d), F32),
                        pltpu.VMEM((CONV_WIDTH, SUBLANES, d), F32),
                        pltpu.VMEM((CONV_TILE, d), F32)],
        compiler_params=_params("arbitrary", "arbitrary"),
        name="conv_core",
    )(u, w_dw, b_dw, ln_g, ln_b, *[stack for stack, _ in weights])


def _hgrn_constants():
    c = HGRN_CHUNK
    t = jnp.arange(c)[:, None]
    s = jnp.arange(c)[None, :]
    tri = (s <= t).astype(BF16)
    masks = []
    for hs in HGRN_LEVELS[1:]:
        same = (t // (2 * hs)) == (s // (2 * hs))
        masks.append(same & ((t // hs) % 2 == 1) & ((s // hs) % 2 == 0))
    masks.append(t == s)
    return tri, jnp.stack(masks).astype(F32)


def _hgrn_kernel(h_ref, gmix_ref, wq_ref, lbp_ref, ng_ref, tri_ref, mask_ref, *refs, layer, n_cast):
    o_ref = refs[n_cast]
    q_s, k_s, b_s, v_s, g_s, st_ref = refs[2 * n_cast + 1:]
    _cast_weights(refs[:n_cast], refs[n_cast + 1:2 * n_cast + 1])
    chunk = HGRN_CHUNK
    blocks = chunk // SUBLANES

    @pl.when(pl.program_id(1) == 0)
    def _():
        st_ref[...] = jnp.zeros(st_ref.shape, F32)

    p = lbp_ref[...]
    e = jnp.exp(p - jnp.max(p, axis=0, keepdims=True))
    sm = e / jnp.sum(e, axis=0, keepdims=True)
    lb = jnp.zeros((1, D_MODEL), F32)
    for l in range(1, layer + 1):
        lb = lb + sm[l:l + 1, :]

    def prepare(c0):
        rows = slice(c0 * chunk, (c0 + PREP_CHUNKS) * chunk)
        hn = _rms_norm(h_ref[rows, :], gmix_ref[...]).astype(BF16)
        q = _silu(_dot(hn, wq_ref[:, 0:D_MODEL])) * (HGRN_DK ** -0.5)
        v = _dot(hn, wq_ref[:, 2 * D_MODEL:3 * D_MODEL]).astype(BF16)
        g = _silu(_dot(hn, wq_ref[:, 3 * D_MODEL:4 * D_MODEL]))
        f = lb + (1.0 - lb) * _sigmoid(_dot(hn, wq_ref[:, D_MODEL:2 * D_MODEL]))
        logf = jnp.log(f)
        g1 = logf.astype(BF16)
        r1 = logf - g1.astype(F32)
        g2 = r1.astype(BF16)
        g3 = (r1 - g2.astype(F32)).astype(BF16)
        tri = tri_ref[...]
        for k in range(PREP_CHUNKS):
            part = slice(k * chunk, (k + 1) * chunk)
            q_s[c0 + k], v_s[c0 + k], g_s[c0 + k] = q[part], v[part], g[part]
            k_s[c0 + k] = 1.0 - f[part]
            b_s[c0 + k] = (_dot(tri, g1[part]) + _dot(tri, g2[part]) + _dot(tri, g3[part])) * LOG2E

    row_in_blk = lax.broadcasted_iota(jnp.int32, (blocks, SUBLANES, HGRN_DK), 1)
    upper_of_8 = row_in_blk >= 4
    sign4 = jnp.where(upper_of_8, 1.0, -1.0)
    sign2 = jnp.where(row_in_blk % 4 >= 2, 1.0, -1.0)

    def scores(c, hd):
        sl = slice(hd * HGRN_DK, (hd + 1) * HGRN_DK)
        qh, kh, bh = q_s[c, :, sl], k_s[c, :, sl], b_s[c, :, sl]
        kb = kh.astype(BF16)
        state = st_ref[hd]

        o = _dot_nt((qh * jnp.exp2(bh)).astype(BF16), state.astype(BF16))

        a = jnp.zeros((chunk, chunk), F32)
        for lvl, hs in enumerate(HGRN_LEVELS[:4]):
            q_parts, k_parts = [], []
            zeros = jnp.zeros((hs, HGRN_DK), F32)
            for s0 in range(0, chunk, 2 * hs):
                pivot = bh[s0 + hs - 1:s0 + hs, :]
                lo, up = slice(s0, s0 + hs), slice(s0 + hs, s0 + 2 * hs)
                q_parts += [zeros, qh[up] * jnp.exp2(bh[up] - pivot)]
                k_parts += [kh[lo] * jnp.exp2(pivot - bh[lo]), zeros]
            al = _dot_nt(jnp.concatenate(q_parts, axis=0).astype(BF16),
                         jnp.concatenate(k_parts, axis=0).astype(BF16))
            a = a + (al if lvl == 0 else al * mask_ref[lvl - 1])

        b3 = bh.reshape(blocks, SUBLANES, HGRN_DK)
        q3 = qh.reshape(blocks, SUBLANES, HGRN_DK)
        k3 = kh.reshape(blocks, SUBLANES, HGRN_DK)
        pivot4 = b3[:, 3:4, :]
        pivot2 = jnp.where(upper_of_8, b3[:, 5:6, :], b3[:, 1:2, :])
        for lvl, w3 in ((4, jnp.exp2((b3 - pivot4) * sign4)), (5, jnp.exp2((b3 - pivot2) * sign2))):
            al = _dot_nt((q3 * w3).reshape(chunk, HGRN_DK).astype(BF16),
                         (k3 * w3).reshape(chunk, HGRN_DK).astype(BF16))
            a = a + al * mask_ref[lvl - 1]
        a = a + _dot_nt((qh * (1.0 - kh)).astype(BF16), kb) * mask_ref[5]
        a = a + _dot_nt(qh.astype(BF16), kb) * mask_ref[6]
        return o, a, state

    def mix(c, hd, o, a, state):
        sl = slice(hd * HGRN_DK, (hd + 1) * HGRN_DK)
        rows = slice(c * chunk, (c + 1) * chunk)
        kh, bh = k_s[c, :, sl], b_s[c, :, sl]
        vb = v_s[c, :, sl]
        b_last = bh[chunk - 1:chunk, :]
        o = o + _dot(a.astype(BF16), vb)
        k_dec = (kh * jnp.exp2(b_last - bh)).astype(BF16)
        st_ref[hd] = state * jnp.exp2(b_last) + _dot(vb.astype(F32).T.astype(BF16), k_dec)
        return o

    def readout(c, hd, o):
        sl = slice(hd * HGRN_DK, (hd + 1) * HGRN_DK)
        rows = slice(c * chunk, (c + 1) * chunk)
        on = o * lax.rsqrt(jnp.mean(o * o, axis=-1, keepdims=True) + NORM_EPS)
        o_ref[rows, sl] = (on * ng_ref[:, sl] * g_s[c, :, sl]).astype(o_ref.dtype)

    items = [(c, hd) for c in range(HGRN_STEP_CHUNKS) for hd in range(HGRN_HEADS)]
    prepare(0)
    stage1, stage2 = {}, {}
    for i in range(len(items) + 2 * HEAD_SKEW):
        if i < len(items):
            stage1[i] = scores(*items[i])
        if 0 <= i - HEAD_SKEW < len(items):
            stage2[i - HEAD_SKEW] = mix(*items[i - HEAD_SKEW], *stage1.pop(i - HEAD_SKEW))
        if 0 <= i - 2 * HEAD_SKEW < len(items):
            readout(*items[i - 2 * HEAD_SKEW], stage2.pop(i - 2 * HEAD_SKEW))
        if 1 <= i < HGRN_STEP_CHUNKS // PREP_CHUNKS:
            prepare(i * PREP_CHUNKS)


def _hgrn_core(h, g_mix, wq, lower_bounds, norm_g, layer, weights, batch, seq):
    n, d = h.shape
    rows = HGRN_STEP_CHUNKS * HGRN_CHUNK
    nc = seq // rows
    tri, masks = _hgrn_constants()
    cast_in, cast_out, cast_shapes = _cast_specs(weights, batch * nc, lambda b, c: b * nc + c)
    chunk_scratch = lambda dtype: pltpu.VMEM((HGRN_STEP_CHUNKS, HGRN_CHUNK, d), dtype)
    return pl.pallas_call(
        functools.partial(_hgrn_kernel, layer=layer, n_cast=len(weights)),
        grid=(batch, nc),
        in_specs=[pl.BlockSpec((rows, d), lambda b, c: (b * nc + c, 0)), _resident((1, d)),
                  _resident(wq.shape),
                  _resident(lower_bounds.shape), _resident((1, d)),
                  _resident(tri.shape), _resident(masks.shape)] + cast_in,
        out_specs=[pl.BlockSpec((rows, d), lambda b, c: (b * nc + c, 0))] + cast_out,
        out_shape=[jax.ShapeDtypeStruct((n, d), BF16)] + cast_shapes,
        scratch_shapes=[chunk_scratch(F32), chunk_scratch(F32), chunk_scratch(F32), chunk_scratch(BF16),
                        chunk_scratch(F32), pltpu.VMEM((HGRN_HEADS, HGRN_DK, HGRN_DK), F32)],
        compiler_params=_params("arbitrary", "arbitrary"),
        name="hgrn_core",
    )(h, g_mix, wq, lower_bounds, norm_g, tri, masks, *[stack for stack, _ in weights])


def _resid_mlp_kernel(h_ref, y_ref, wo_ref, bo_ref, g_ref, w1_ref, w2_ref, gf_ref, o_ref, *,
                      has_bias, final_norm):
    h2 = h_ref[...] + _dot(y_ref[...], wo_ref[...])
    if has_bias:
        h2 = h2 + bo_ref[...]
    hn = _rms_norm(h2, g_ref[...]).astype(BF16)
    acc = h2
    for c in range(0, D_FF, FF_CHUNK):
        up = jnp.maximum(_dot(hn, w1_ref[:, c:c + FF_CHUNK]), 0.0)
        acc = acc + _dot((up * up).astype(BF16), w2_ref[c:c + FF_CHUNK, :])
    if final_norm:
        acc = _rms_norm(acc, gf_ref[...])
    o_ref[...] = acc


def _resid_mlp(h, y, wo, bo, g, w1, w2, gf, has_bias, final_norm):
    n, d = h.shape
    row = lambda i: (i, 0)
    return pl.pallas_call(
        functools.partial(_resid_mlp_kernel, has_bias=has_bias, final_norm=final_norm),
        grid=(n // ROW_TILE,),
        in_specs=[pl.BlockSpec((ROW_TILE, d), row), pl.BlockSpec((ROW_TILE, d), row),
                  _resident(wo.shape), _resident((1, d)), _resident((1, d)),
                  _resident(w1.shape), _resident(w2.shape), _resident((1, d))],
        out_specs=pl.BlockSpec((ROW_TILE, d), row),
        out_shape=jax.ShapeDtypeStruct((n, d), F32),
        compiler_params=_params("parallel"),
        name="resid_mlp",
    )(h, y, wo, bo, g, w1, w2, gf)


def kernel(x, attn_w_qkv, attn_sinks, attn_w_o, conv_w_pw1, conv_b_pw1, conv_w_dw, conv_b_dw, conv_ln_g,
           conv_ln_b, conv_w_pw2, conv_b_pw2, hgrn_w_qfig, hgrn_lower_bounds, hgrn_norm_g, hgrn_w_o,
           norm_mixer, norm_mlp, mlp_w1, mlp_w2, final_norm):
    batch, seq, d = x.shape
    h = x.reshape(batch * seq, d)
    row = lambda v: v.reshape(1, -1)
    zero_bias = jnp.zeros((1, d), F32)
    stacks = {0: attn_w_qkv, 1: conv_w_pw1, 2: hgrn_w_qfig}
    w_in = attn_w_qkv[0].astype(BF16)
    for i in range(DEPTH):
        kind, j = i % N_MIXERS, i // N_MIXERS
        g_mix = row(norm_mixer[i])
        bias, has_bias = zero_bias, False
        weights = [(mlp_w1, i), (mlp_w2, i)]
        if i + 1 < DEPTH:
            weights.append((stacks[(i + 1) % N_MIXERS], (i + 1) // N_MIXERS))
        if kind == 0:
            qkv = _norm_proj(h, g_mix, w_in, BF16)
            y, wo, w1, w2, *nxt = _attention_core(qkv, attn_sinks[j], [(attn_w_o, j)] + weights, batch, seq)
        elif kind == 1:
            u = _norm_glu(h, g_mix, w_in, row(conv_b_pw1[j]))
            y, wo, w1, w2, *nxt = _conv_core(u, conv_w_dw[j], row(conv_b_dw[j]), row(conv_ln_g[j]),
                                             row(conv_ln_b[j]), [(conv_w_pw2, j)] + weights, batch, seq)
            bias, has_bias = row(conv_b_pw2[j]), True
        else:
            y, wo, w1, w2, *nxt = _hgrn_core(h, g_mix, w_in, hgrn_lower_bounds, row(hgrn_norm_g[j]), i,
                                             [(hgrn_w_o, j)] + weights, batch, seq)
        h = _resid_mlp(h, y, wo, bias, row(norm_mlp[i]), w1, w2, row(final_norm), has_bias, i == DEPTH - 1)
        w_in = nxt[0] if nxt else None
    return h.reshape(batch, seq, d)
```

```python
import functools
import math

import jax
import jax.numpy as jnp
from jax import lax
from jax.experimental import pallas as pl
from jax.experimental.pallas import tpu as pltpu

F32 = jnp.float32
BF16 = jnp.bfloat16

D_MODEL = 1024
DEPTH = 4
N_MIXERS = 3
ATTN_HEADS = 16
ATTN_KV_HEADS = 2
HEAD_DIM = 64
WINDOW = 128
QKV_DIM = (ATTN_HEADS + 2 * ATTN_KV_HEADS) * HEAD_DIM
CONV_WIDTH = 31
HGRN_HEADS = 8
HGRN_DK = 128
D_FF = 4 * D_MODEL
NORM_EPS = 1e-6
MASK_VALUE = -1e30
LOG2E = math.log2(math.e)

V7X_VMEM_BYTES = 64 * 1024 * 1024
VMEM_LIMIT_BYTES = V7X_VMEM_BYTES - 8 * 1024 * 1024
SUBLANES = 8
LANES = 128

ROW_TILE = 1024
FF_CHUNK = 1024
ATTN_TILE = 1024
CONV_TILE = 512
CONV_HALO = 32
CONV_ROWS = 32
CONV_LANES = 128
NORM_ROWS = 512
HGRN_CHUNK = 128
HGRN_STEP_CHUNKS = 4
PREP_CHUNKS = 4
HGRN_LEVELS = (64, 32, 16, 8, 4, 2, 1)
HEAD_SKEW = 4


def _params(*semantics):
    return pltpu.CompilerParams(dimension_semantics=semantics, vmem_limit_bytes=VMEM_LIMIT_BYTES)


def _resident(shape):
    return pl.BlockSpec(shape, lambda *_: (0,) * len(shape), pipeline_mode=pl.Buffered(1))


def _rms_norm(x, g):
    ms = jnp.mean(x * x, axis=-1, keepdims=True)
    return x * lax.rsqrt(ms + NORM_EPS) * g


def _sigmoid(x):
    return 0.5 * jnp.tanh(0.5 * x) + 0.5


def _silu(x):
    h = 0.5 * x
    return h + h * jnp.tanh(h)


def _zero_like(x):
    bits = lax.bitcast_convert_type(x, jnp.uint32)
    sixteen = jnp.uint32(16)
    return lax.bitcast_convert_type((bits >> sixteen) >> sixteen, F32)


def _cast_specs(weights, n_steps, step_index):
    in_specs, out_specs, out_shapes = [], [], []
    for stack, layer in weights:
        _, rows, cols = stack.shape
        part = rows // n_steps
        in_specs.append(pl.BlockSpec((None, part, cols), lambda *g, layer=layer: (layer, step_index(*g), 0)))
        out_specs.append(pl.BlockSpec((part, cols), lambda *g: (step_index(*g), 0)))
        out_shapes.append(jax.ShapeDtypeStruct((rows, cols), BF16))
    return in_specs, out_specs, out_shapes


def _cast_weights(src_refs, dst_refs):
    for src, dst in zip(src_refs, dst_refs):
        dst[...] = src[...].astype(BF16)


def _dot(a, b):
    return jnp.dot(a, b, preferred_element_type=F32)


def _dot_nt(a, b):
    return lax.dot_general(a, b, (((1,), (1,)), ((), ())), preferred_element_type=F32)


def _norm_glu_kernel(x_ref, g_ref, w_ref, b_ref, o_ref):
    hn = _rms_norm(x_ref[...], g_ref[...]).astype(BF16)
    a = _dot(hn, w_ref[:, :D_MODEL]) + b_ref[:, :D_MODEL]
    gate = _dot(hn, w_ref[:, D_MODEL:]) + b_ref[:, D_MODEL:]
    o_ref[...] = a * _sigmoid(gate)


def _norm_glu(h, g, w, b):
    n, d = h.shape
    return pl.pallas_call(
        _norm_glu_kernel,
        grid=(n // ROW_TILE,),
        in_specs=[pl.BlockSpec((ROW_TILE, d), lambda i: (i, 0)), _resident((1, d)),
                  _resident(w.shape), _resident((1, 2 * d))],
        out_specs=pl.BlockSpec((ROW_TILE, d), lambda i: (i, 0)),
        out_shape=jax.ShapeDtypeStruct((n, d), F32),
        compiler_params=_params("parallel"),
        name="norm_glu",
    )(h, g, w, b)


def _attn_bias():
    qi = jnp.arange(WINDOW)[:, None]
    col = jnp.arange(4 * WINDOW)[None, :]
    si = (col // (2 * WINDOW)) * WINDOW + col % WINDOW
    second = (col // WINDOW) % 2
    dist = WINDOW + qi - si
    valid = (dist >= 0) & (dist < WINDOW)
    head = 2 * jnp.arange(ATTN_HEADS // 2)[:, None, None] + second[None]
    slopes = jnp.exp2(-8.0 * (head + 1).astype(F32) / ATTN_HEADS)
    return jnp.where(valid[None], -(slopes * dist[None].astype(F32)), MASK_VALUE)


def _attn_kernel(sink_ref, h_ref, gmix_ref, wqkv_ref, bias_ref, *refs, n_cast):
    o_ref = refs[n_cast]
    qkv_s, kv_prev_s = refs[2 * n_cast + 1:]
    _cast_weights(refs[:n_cast], refs[n_cast + 1:2 * n_cast + 1])
    first_tile = pl.program_id(1) == 0
    q_cols = ATTN_HEADS * HEAD_DIM

    @pl.when(first_tile)
    def _():
        kv_prev_s[...] = jnp.zeros(kv_prev_s.shape, BF16)

    hn = _rms_norm(h_ref[...], gmix_ref[...]).astype(BF16)
    qkv_s[...] = _dot(hn, wqkv_ref[...]).astype(BF16)
    kv_prev = kv_prev_s[...]
    kv_prev_s[...] = qkv_s[ATTN_TILE - WINDOW:, q_cols:]
    pair = 2 * HEAD_DIM
    nblk = ATTN_TILE // WINDOW
    group = ATTN_HEADS // ATTN_KV_HEADS
    scale = HEAD_DIM ** -0.5
    low = lax.broadcasted_iota(jnp.int32, (WINDOW, pair), 1) < HEAD_DIM
    ones_bd = jnp.concatenate([jnp.where(low, 1.0, 0.0), jnp.where(low, 0.0, 1.0)], axis=0).astype(BF16)
    prev_cols = lax.broadcasted_iota(jnp.int32, (WINDOW, 4 * WINDOW), 1) < 2 * WINDOW

    def expand(x, xr, kvh):
        first, second = (x, xr) if kvh == 0 else (xr, x)
        return jnp.concatenate([jnp.where(low, first, 0.0), jnp.where(low, 0.0, second)], axis=0).astype(BF16)

    kblk, vblk = [], []
    for r in range(nblk + 1):
        kv = (kv_prev if r == 0 else qkv_s[(r - 1) * WINDOW:r * WINDOW, q_cols:]).astype(F32)
        k2, v2 = kv[:, :pair] * scale, kv[:, pair:]
        k2r, v2r = pltpu.roll(k2, HEAD_DIM, 1), pltpu.roll(v2, HEAD_DIM, 1)
        kblk.append([expand(k2, k2r, kvh) for kvh in range(ATTN_KV_HEADS)])
        vblk.append([jnp.concatenate([expand(v2, v2r, kvh), ones_bd], axis=1) for kvh in range(ATTN_KV_HEADS)])

    kb = [[jnp.concatenate([kblk[r][kvh], kblk[r + 1][kvh]], axis=0) for kvh in range(ATTN_KV_HEADS)]
          for r in range(nblk)]
    vb = [[jnp.concatenate([vblk[r][kvh], vblk[r + 1][kvh]], axis=0) for kvh in range(ATTN_KV_HEADS)]
          for r in range(nblk)]

    def scores(r, j):
        rows, cols = slice(r * WINDOW, (r + 1) * WINDOW), slice(j * pair, (j + 1) * pair)
        s = _dot_nt(qkv_s[rows, cols], kb[r][(2 * j) // group]) + bias_ref[j]
        if r == 0:
            s = jnp.where(first_tile & prev_cols, MASK_VALUE, s)
        return s

    def weigh(r, j, s):
        rows, cols = slice(r * WINDOW, (r + 1) * WINDOW), slice(j * pair, (j + 1) * pair)
        sa0, sb0, sa1, sb1 = (s[:, i * WINDOW:(i + 1) * WINDOW] for i in range(4))
        sink_a, sink_b = sink_ref[2 * j], sink_ref[2 * j + 1]
        ma = jnp.maximum(jnp.max(jnp.maximum(sa0, sa1), axis=-1, keepdims=True), sink_a)
        mb = jnp.maximum(jnp.max(jnp.maximum(sb0, sb1), axis=-1, keepdims=True), sink_b)
        e = jnp.concatenate([jnp.exp(sa0 - ma), jnp.exp(sb0 - mb), jnp.exp(sa1 - ma), jnp.exp(sb1 - mb)],
                            axis=1).astype(BF16)
        ov = _dot(e, vb[r][(2 * j) // group])
        sink_term = jnp.where(low, jnp.exp(sink_a - ma), jnp.exp(sink_b - mb))
        o_ref[rows, cols] = (ov[:, :pair] / (ov[:, pair:] + sink_term)).astype(o_ref.dtype)

    for r in range(nblk):
        for j in range(ATTN_HEADS // 2):
            weigh(r, j, scores(r, j))


def _attention_core(h, g_mix, w_qkv, sinks, weights, batch, seq):
    n, d = h.shape
    nt = seq // ATTN_TILE
    q_cols = ATTN_HEADS * HEAD_DIM
    kv_cols = 2 * ATTN_KV_HEADS * HEAD_DIM
    bias = _attn_bias()
    cast_in, cast_out, cast_shapes = _cast_specs(weights, batch * nt, lambda b, i: b * nt + i)
    return pl.pallas_call(
        functools.partial(_attn_kernel, n_cast=len(weights)),
        grid=(batch, nt),
        in_specs=[
            pl.BlockSpec(memory_space=pltpu.SMEM),
            pl.BlockSpec((ATTN_TILE, d), lambda b, i: (b * nt + i, 0)),
            _resident((1, d)), _resident(w_qkv.shape), _resident(bias.shape),
        ] + cast_in,
        out_specs=[pl.BlockSpec((ATTN_TILE, q_cols), lambda b, i: (b * nt + i, 0))] + cast_out,
        out_shape=[jax.ShapeDtypeStruct((n, q_cols), BF16)] + cast_shapes,
        scratch_shapes=[pltpu.VMEM((ATTN_TILE, QKV_DIM), BF16), pltpu.VMEM((WINDOW, kv_cols), BF16)],
        compiler_params=_params("arbitrary", "arbitrary"),
        name="attn_core",
    )(sinks, h, g_mix, w_qkv, bias, *[stack for stack, _ in weights])


def _conv_kernel(u_ref, w_ref, bdw_ref, lng_ref, lnb_ref, *refs, n_cast):
    o_ref = refs[n_cast]
    buf_ref, wb_ref, c_ref = refs[2 * n_cast + 1:]
    _cast_weights(refs[:n_cast], refs[n_cast + 1:2 * n_cast + 1])
    t = pl.program_id(1)
    tile = u_ref.shape[0]

    @pl.when((pl.program_id(0) == 0) & (t == 0))
    def _():
        for j in range(CONV_WIDTH):
            wb_ref[j] = jnp.broadcast_to(w_ref[j:j + 1, :], (SUBLANES, D_MODEL))

    @pl.when(t == 0)
    def _():
        buf_ref[0, 0:CONV_HALO, :] = jnp.zeros((CONV_HALO, D_MODEL), F32)

    @pl.when(t > 0)
    def _():
        buf_ref[0, 0:CONV_HALO, :] = buf_ref[0, tile:tile + CONV_HALO, :]

    buf_ref[0, CONV_HALO:, :] = u_ref[...]
    shifted_rows = tile + CONV_HALO - SUBLANES
    for r in range(1, SUBLANES):
        buf_ref[r, 0:shifted_rows, :] = buf_ref[0, r:r + shifted_rows, :]

    first_tap_row = CONV_HALO - (CONV_WIDTH - 1)
    groups = CONV_ROWS // SUBLANES

    def conv_rows(i, carry):
        base = pl.multiple_of(i * CONV_ROWS, CONV_ROWS)
        acc = None
        for l in range(0, D_MODEL, CONV_LANES):
            lanes = slice(l, l + CONV_LANES)
            start = bdw_ref[:, lanes]
            if acc is not None:
                start = start + _zero_like(acc[0])
            acc = jnp.broadcast_to(start, (groups, SUBLANES, CONV_LANES))
            for r in range(SUBLANES):
                taps = [j for j in range(CONV_WIDTH) if (first_tap_row + j) % SUBLANES == r]
                a0 = (first_tap_row + taps[0]) // SUBLANES
                span = (first_tap_row + taps[-1]) // SUBLANES - a0 + groups
                big = buf_ref[r, pl.ds(base + a0 * SUBLANES, span * SUBLANES), lanes]
                big = big.reshape(span, SUBLANES, CONV_LANES)
                for j in taps:
                    a = (first_tap_row + j) // SUBLANES - a0
                    acc = acc + wb_ref[j, :, lanes] * big[a:a + groups]
            c_ref[pl.ds(base, CONV_ROWS), lanes] = acc.reshape(CONV_ROWS, CONV_LANES)
        return carry

    lax.fori_loop(0, tile // CONV_ROWS, conv_rows, 0)

    def norm_rows(i, carry):
        base = pl.multiple_of(i * NORM_ROWS, NORM_ROWS)
        c = c_ref[pl.ds(base, NORM_ROWS), :]
        mu = jnp.mean(c, axis=-1, keepdims=True)
        cen = c - mu
        var = jnp.mean(cen * cen, axis=-1, keepdims=True)
        y = cen * lax.rsqrt(var + NORM_EPS) * lng_ref[...] + lnb_ref[...]
        o_ref[pl.ds(base, NORM_ROWS), :] = _silu(y).astype(o_ref.dtype)
        return carry

    lax.fori_loop(0, tile // NORM_ROWS, norm_rows, 0)


def _conv_core(u, w_dw, b_dw, ln_g, ln_b, weights, batch, seq):
    n, d = u.shape
    nt = seq // CONV_TILE
    row = lambda b, t: (b * nt + t, 0)
    cast_in, cast_out, cast_shapes = _cast_specs(weights, batch * nt, lambda b, t: b * nt + t)
    return pl.pallas_call(
        functools.partial(_conv_kernel, n_cast=len(weights)),
        grid=(batch, nt),
        in_specs=[pl.BlockSpec((CONV_TILE, d), row), _resident((CONV_WIDTH, d)),
                  _resident((1, d)), _resident((1, d)), _resident((1, d))] + cast_in,
        out_specs=[pl.BlockSpec((CONV_TILE, d), row)] + cast_out,
        out_shape=[jax.ShapeDtypeStruct((n, d), BF16)] + cast_shapes,
        scratch_shapes=[pltpu.VMEM((SUBLANES, CONV_TILE + CONV_HALO, d), F32),
                        pltpu.VMEM((CONV_WIDTH, SUBLANES, d), F32),
                        pltpu.VMEM((CONV_TILE, d), F32)],
        compiler_params=_params("arbitrary", "arbitrary"),
        name="conv_core",
    )(u, w_dw, b_dw, ln_g, ln_b, *[stack for stack, _ in weights])


def _hgrn_constants():
    c = HGRN_CHUNK
    t = jnp.arange(c)[:, None]
    s = jnp.arange(c)[None, :]
    tri = (s <= t).astype(BF16)
    masks = []
    for hs in HGRN_LEVELS[1:]:
        same = (t // (2 * hs)) == (s // (2 * hs))
        masks.append(same & ((t // hs) % 2 == 1) & ((s // hs) % 2 == 0))
    masks.append(t == s)
    return tri, jnp.stack(masks).astype(F32)


def _hgrn_kernel(h_ref, gmix_ref, wq_ref, lbp_ref, ng_ref, tri_ref, mask_ref, *refs, layer, n_cast):
    o_ref = refs[n_cast]
    q_s, k_s, b_s, v_s, g_s, st_ref = refs[2 * n_cast + 1:]
    _cast_weights(refs[:n_cast], refs[n_cast + 1:2 * n_cast + 1])
    chunk = HGRN_CHUNK
    blocks = chunk // SUBLANES

    @pl.when(pl.program_id(1) == 0)
    def _():
        st_ref[...] = jnp.zeros(st_ref.shape, F32)

    p = lbp_ref[...]
    e = jnp.exp(p - jnp.max(p, axis=0, keepdims=True))
    sm = e / jnp.sum(e, axis=0, keepdims=True)
    lb = jnp.zeros((1, D_MODEL), F32)
    for l in range(1, layer + 1):
        lb = lb + sm[l:l + 1, :]

    def prepare(c0):
        rows = slice(c0 * chunk, (c0 + PREP_CHUNKS) * chunk)
        hn = _rms_norm(h_ref[rows, :], gmix_ref[...]).astype(BF16)
        q = _silu(_dot(hn, wq_ref[:, 0:D_MODEL])) * (HGRN_DK ** -0.5)
        v = _dot(hn, wq_ref[:, 2 * D_MODEL:3 * D_MODEL]).astype(BF16)
        g = _silu(_dot(hn, wq_ref[:, 3 * D_MODEL:4 * D_MODEL]))
        f = lb + (1.0 - lb) * _sigmoid(_dot(hn, wq_ref[:, D_MODEL:2 * D_MODEL]))
        logf = jnp.log(f)
        g1 = logf.astype(BF16)
        r1 = logf - g1.astype(F32)
        g2 = r1.astype(BF16)
        g3 = (r1 - g2.astype(F32)).astype(BF16)
        tri = tri_ref[...]
        for k in range(PREP_CHUNKS):
            part = slice(k * chunk, (k + 1) * chunk)
            q_s[c0 + k], v_s[c0 + k], g_s[c0 + k] = q[part], v[part], g[part]
            k_s[c0 + k] = 1.0 - f[part]
            b_s[c0 + k] = (_dot(tri, g1[part]) + _dot(tri, g2[part]) + _dot(tri, g3[part])) * LOG2E

    row_in_blk = lax.broadcasted_iota(jnp.int32, (blocks, SUBLANES, HGRN_DK), 1)
    upper_of_8 = row_in_blk >= 4
    sign4 = jnp.where(upper_of_8, 1.0, -1.0)
    sign2 = jnp.where(row_in_blk % 4 >= 2, 1.0, -1.0)

    def scores(c, hd):
        sl = slice(hd * HGRN_DK, (hd + 1) * HGRN_DK)
        qh, kh, bh = q_s[c, :, sl], k_s[c, :, sl], b_s[c, :, sl]
        kb = kh.astype(BF16)
        state = st_ref[hd]

        o = _dot_nt((qh * jnp.exp2(bh)).astype(BF16), state.astype(BF16))

        a = jnp.zeros((chunk, chunk), F32)
        for lvl, hs in enumerate(HGRN_LEVELS[:4]):
            q_parts, k_parts = [], []
            zeros = jnp.zeros((hs, HGRN_DK), F32)
            for s0 in range(0, chunk, 2 * hs):
                pivot = bh[s0 + hs - 1:s0 + hs, :]
                lo, up = slice(s0, s0 + hs), slice(s0 + hs, s0 + 2 * hs)
                q_parts += [zeros, qh[up] * jnp.exp2(bh[up] - pivot)]
                k_parts += [kh[lo] * jnp.exp2(pivot - bh[lo]), zeros]
            al = _dot_nt(jnp.concatenate(q_parts, axis=0).astype(BF16),
                         jnp.concatenate(k_parts, axis=0).astype(BF16))
            a = a + (al if lvl == 0 else al * mask_ref[lvl - 1])

        b3 = bh.reshape(blocks, SUBLANES, HGRN_DK)
        q3 = qh.reshape(blocks, SUBLANES, HGRN_DK)
        k3 = kh.reshape(blocks, SUBLANES, HGRN_DK)
        pivot4 = b3[:, 3:4, :]
        pivot2 = jnp.where(upper_of_8, b3[:, 5:6, :], b3[:, 1:2, :])
        for lvl, w3 in ((4, jnp.exp2((b3 - pivot4) * sign4)), (5, jnp.exp2((b3 - pivot2) * sign2))):
            al = _dot_nt((q3 * w3).reshape(chunk, HGRN_DK).astype(BF16),
                         (k3 * w3).reshape(chunk, HGRN_DK).astype(BF16))
            a = a + al * mask_ref[lvl - 1]
        a = a + _dot_nt((qh * (1.0 - kh)).astype(BF16), kb) * mask_ref[5]
        a = a + _dot_nt(qh.astype(BF16), kb) * mask_ref[6]
        return o, a, state

    def mix(c, hd, o, a, state):
        sl = slice(hd * HGRN_DK, (hd + 1) * HGRN_DK)
        rows = slice(c * chunk, (c + 1) * chunk)
        kh, bh = k_s[c, :, sl], b_s[c, :, sl]
        vb = v_s[c, :, sl]
        b_last = bh[chunk - 1:chunk, :]
        o = o + _dot(a.astype(BF16), vb)
        k_dec = (kh * jnp.exp2(b_last - bh)).astype(BF16)
        st_ref[hd] = state * jnp.exp2(b_last) + _dot(vb.astype(F32).T.astype(BF16), k_dec)
        return o

    def readout(c, hd, o):
        sl = slice(hd * HGRN_DK, (hd + 1) * HGRN_DK)
        rows = slice(c * chunk, (c + 1) * chunk)
        on = o * lax.rsqrt(jnp.mean(o * o, axis=-1, keepdims=True) + NORM_EPS)
        o_ref[rows, sl] = (on * ng_ref[:, sl] * g_s[c, :, sl]).astype(o_ref.dtype)

    items = [(c, hd) for c in range(HGRN_STEP_CHUNKS) for hd in range(HGRN_HEADS)]
    prepare(0)
    stage1, stage2 = {}, {}
    for i in range(len(items) + 2 * HEAD_SKEW):
        if i < len(items):
            stage1[i] = scores(*items[i])
        if 0 <= i - HEAD_SKEW < len(items):
            stage2[i - HEAD_SKEW] = mix(*items[i - HEAD_SKEW], *stage1.pop(i - HEAD_SKEW))
        if 0 <= i - 2 * HEAD_SKEW < len(items):
            readout(*items[i - 2 * HEAD_SKEW], stage2.pop(i - 2 * HEAD_SKEW))
        if 1 <= i < HGRN_STEP_CHUNKS // PREP_CHUNKS:
            prepare(i * PREP_CHUNKS)


def _hgrn_core(h, g_mix, wq, lower_bounds, norm_g, layer, weights, batch, seq):
    n, d = h.shape
    rows = HGRN_STEP_CHUNKS * HGRN_CHUNK
    nc = seq // rows
    tri, masks = _hgrn_constants()
    cast_in, cast_out, cast_shapes = _cast_specs(weights, batch * nc, lambda b, c: b * nc + c)
    chunk_scratch = lambda dtype: pltpu.VMEM((HGRN_STEP_CHUNKS, HGRN_CHUNK, d), dtype)
    return pl.pallas_call(
        functools.partial(_hgrn_kernel, layer=layer, n_cast=len(weights)),
        grid=(batch, nc),
        in_specs=[pl.BlockSpec((rows, d), lambda b, c: (b * nc + c, 0)), _resident((1, d)),
                  _resident(wq.shape),
                  _resident(lower_bounds.shape), _resident((1, d)),
                  _resident(tri.shape), _resident(masks.shape)] + cast_in,
        out_specs=[pl.BlockSpec((rows, d), lambda b, c: (b * nc + c, 0))] + cast_out,
        out_shape=[jax.ShapeDtypeStruct((n, d), BF16)] + cast_shapes,
        scratch_shapes=[chunk_scratch(F32), chunk_scratch(F32), chunk_scratch(F32), chunk_scratch(BF16),
                        chunk_scratch(F32), pltpu.VMEM((HGRN_HEADS, HGRN_DK, HGRN_DK), F32)],
        compiler_params=_params("arbitrary", "arbitrary"),
        name="hgrn_core",
    )(h, g_mix, wq, lower_bounds, norm_g, tri, masks, *[stack for stack, _ in weights])


def _resid_mlp_kernel(h_ref, y_ref, wo_ref, bo_ref, g_ref, w1_ref, w2_ref, gf_ref, o_ref, *,
                      has_bias, final_norm):
    h2 = h_ref[...] + _dot(y_ref[...], wo_ref[...])
    if has_bias:
        h2 = h2 + bo_ref[...]
    hn = _rms_norm(h2, g_ref[...]).astype(BF16)
    acc = h2
    for c in range(0, D_FF, FF_CHUNK):
        up = jnp.maximum(_dot(hn, w1_ref[:, c:c + FF_CHUNK]), 0.0)
        acc = acc + _dot((up * up).astype(BF16), w2_ref[c:c + FF_CHUNK, :])
    if final_norm:
        acc = _rms_norm(acc, gf_ref[...])
    o_ref[...] = acc


def _resid_mlp(h, y, wo, bo, g, w1, w2, gf, has_bias, final_norm):
    n, d = h.shape
    row = lambda i: (i, 0)
    return pl.pallas_call(
        functools.partial(_resid_mlp_kernel, has_bias=has_bias, final_norm=final_norm),
        grid=(n // ROW_TILE,),
        in_specs=[pl.BlockSpec((ROW_TILE, d), row), pl.BlockSpec((ROW_TILE, d), row),
                  _resident(wo.shape), _resident((1, d)), _resident((1, d)),
                  _resident(w1.shape), _resident(w2.shape), _resident((1, d))],
        out_specs=pl.BlockSpec((ROW_TILE, d), row),
        out_shape=jax.ShapeDtypeStruct((n, d), F32),
        compiler_params=_params("parallel"),
        name="resid_mlp",
    )(h, y, wo, bo, g, w1, w2, gf)


def kernel(x, attn_w_qkv, attn_sinks, attn_w_o, conv_w_pw1, conv_b_pw1, conv_w_dw, conv_b_dw, conv_ln_g,
           conv_ln_b, conv_w_pw2, conv_b_pw2, hgrn_w_qfig, hgrn_lower_bounds, hgrn_norm_g, hgrn_w_o,
           norm_mixer, norm_mlp, mlp_w1, mlp_w2, final_norm):
    batch, seq, d = x.shape
    h = x.reshape(batch * seq, d)
    row = lambda v: v.reshape(1, -1)
    zero_bias = jnp.zeros((1, d), F32)
    stacks = {0: attn_w_qkv, 1: conv_w_pw1, 2: hgrn_w_qfig}
    w_in = attn_w_qkv[0].astype(BF16)
    for i in range(DEPTH):
        kind, j = i % N_MIXERS, i // N_MIXERS
        g_mix = row(norm_mixer[i])
        bias, has_bias = zero_bias, False
        weights = [(mlp_w1, i), (mlp_w2, i)]
        if i + 1 < DEPTH:
            weights.append((stacks[(i + 1) % N_MIXERS], (i + 1) // N_MIXERS))
        if kind == 0:
            y, wo, w1, w2, *nxt = _attention_core(h, g_mix, w_in, attn_sinks[j], [(attn_w_o, j)] + weights,
                                                  batch, seq)
        elif kind == 1:
            u = _norm_glu(h, g_mix, w_in, row(conv_b_pw1[j]))
            y, wo, w1, w2, *nxt = _conv_core(u, conv_w_dw[j], row(conv_b_dw[j]), row(conv_ln_g[j]),
                                             row(conv_ln_b[j]), [(conv_w_pw2, j)] + weights, batch, seq)
            bias, has_bias = row(conv_b_pw2[j]), True
        else:
            y, wo, w1, w2, *nxt = _hgrn_core(h, g_mix, w_in, hgrn_lower_bounds, row(hgrn_norm_g[j]), i,
                                             [(hgrn_w_o, j)] + weights, batch, seq)
        h = _resid_mlp(h, y, wo, bias, row(norm_mlp[i]), w1, w2, row(final_norm), has_bias, i == DEPTH - 1)
        w_in = nxt[0] if nxt else None
    return h.reshape(batch, seq, d)
```
